```python
import jax, jax.numpy as jnp
from jax import lax
import numpy as np

D_MODEL = 2048
BATCH = 2
SEQ = 8192
DEPTH = 1

GLA_HEADS = 4
GLA_DK = 128
GLA_DV = 256
GLA_WIDTH = GLA_HEADS * GLA_DV
GLA_KW = GLA_HEADS * GLA_DK
GLA_GATE_RANK = 16
GLA_GATE_NORM = 16.0
GLA_CHUNK = 64

LRU_HEADS = 4
LRU_WIDTH = 1024
LRU_BLOCK = LRU_WIDTH // LRU_HEADS
LRU_CONV = 4
LRU_C = 8.0

MIX_WIDTH = GLA_WIDTH + LRU_WIDTH
IN_COLS = GLA_KW + GLA_KW + GLA_WIDTH + GLA_WIDTH + GLA_GATE_RANK + LRU_WIDTH + LRU_WIDTH

N_EXPERTS = 32
TOP_K = 4
D_FF = D_MODEL
SWIGLU_LIMIT = 7.0
SWIGLU_ALPHA = 1.702
MOE_BLOCK = 256

EPS = 1e-6

kernel_name = "hymba_gla_rglru_moe_layer"


def rms_norm(x, g):
    xf = x.astype(jnp.float32)
    y = xf * lax.rsqrt(jnp.mean(xf * xf, axis=-1, keepdims=True) + EPS)
    return (y * g.astype(jnp.float32)).astype(x.dtype)


def gla_chunked(q, k, v, log_g):
    B, S, H, DK = q.shape
    DV = v.shape[-1]
    N = S // GLA_CHUNK

    def chunks(t):
        return t.astype(jnp.float32).reshape(B, N, GLA_CHUNK, H, t.shape[-1]).transpose(1, 0, 3, 2, 4)

    q, k, v, log_g = chunks(q) * (DK ** -0.5), chunks(k), chunks(v), chunks(log_g)
    b = jnp.cumsum(log_g, axis=3)
    b_last = b[:, :, :, -1:, :]
    qe = q * jnp.exp(b)
    ke = k * jnp.exp(-b)
    kd = k * jnp.exp(b_last - b)
    causal = jnp.tril(jnp.ones((GLA_CHUNK, GLA_CHUNK), dtype=bool))
    scores = jnp.where(causal, jnp.einsum('nbhik,nbhjk->nbhij', qe, ke), 0.0)
    o_intra = jnp.einsum('nbhij,nbhjv->nbhiv', scores, v)
    decay = jnp.exp(b_last[:, :, :, 0, :])

    def step(state, inp):
        qe_n, kd_n, v_n, dec_n = inp
        o_inter = jnp.einsum('bhik,bhkv->bhiv', qe_n, state)
        state = dec_n[..., None] * state + jnp.einsum('bhjk,bhjv->bhkv', kd_n, v_n)
        return state, o_inter

    s0 = jnp.zeros((B, H, DK, DV), jnp.float32)
    _, o_inter = lax.scan(step, s0, (qe, kd, v, decay))
    o = o_intra + o_inter
    return o.transpose(1, 0, 3, 2, 4).reshape(B, S, H, DV)


def causal_depthwise_conv(x, w, b):
    S = x.shape[1]
    xp = jnp.pad(x, ((0, 0), (LRU_CONV - 1, 0), (0, 0)))
    y = b
    for i in range(LRU_CONV):
        y = y + xp[:, i:i + S, :] * w[i]
    return y


def rg_lru(xc, w_a, b_a, w_i, b_i, a_param):
    B, S, W = xc.shape
    xf = xc.astype(jnp.float32)
    xh = xf.reshape(B, S, LRU_HEADS, LRU_BLOCK)
    gate_a = jax.nn.sigmoid(jnp.einsum('bshi,hij->bshj', xh, w_a).reshape(B, S, W) + b_a)
    gate_i = jax.nn.sigmoid(jnp.einsum('bshi,hij->bshj', xh, w_i).reshape(B, S, W) + b_i)
    log_a = -LRU_C * gate_a * jax.nn.softplus(-a_param.astype(jnp.float32))
    a = jnp.exp(log_a)
    u = xf * gate_i * jnp.sqrt(-jnp.expm1(2.0 * log_a))

    def combine(left, right):
        a_l, u_l = left
        a_r, u_r = right
        return a_l * a_r, a_r * u_l + u_r

    _, h = lax.associative_scan(combine, (a, u), axis=1)
    return h


def moe_ffn(h, w_router, b_router, w_gate_up, b_gate_up, w_down, b_down):
    B, S, D = h.shape
    T = B * S
    TK = T * TOP_K
    xt = h.reshape(T, D)
    logits = (xt @ w_router + b_router).astype(jnp.float32)
    top_vals, top_idx = lax.top_k(logits, TOP_K)
    gates = jax.nn.softmax(top_vals, axis=-1)

    flat_e = top_idx.reshape(-1)
    order = jnp.argsort(flat_e)
    sorted_e = flat_e[order]
    sorted_tok = order // TOP_K
    sorted_w = gates.reshape(-1)[order]
    counts = jnp.bincount(flat_e, length=N_EXPERTS)
    starts = jnp.cumsum(counts) - counts
    padded = ((counts + MOE_BLOCK - 1) // MOE_BLOCK) * MOE_BLOCK
    pstarts = jnp.cumsum(padded) - padded
    dest = pstarts[sorted_e] + (jnp.arange(TK) - starts[sorted_e])
    n_blocks = -(-TK // MOE_BLOCK) + N_EXPERTS
    buf = jnp.zeros((n_blocks * MOE_BLOCK, D), xt.dtype).at[dest].set(xt[sorted_tok])
    block_e = jnp.clip(jnp.searchsorted(pstarts, jnp.arange(n_blocks) * MOE_BLOCK, side='right') - 1,
                       0, N_EXPERTS - 1)

    def expert_block(args):
        xb, e = args
        gu = xb @ w_gate_up[e] + b_gate_up[e]
        gate = jnp.minimum(gu[:, :D_FF], SWIGLU_LIMIT)
        up = jnp.clip(gu[:, D_FF:], -SWIGLU_LIMIT, SWIGLU_LIMIT)
        glu = gate * jax.nn.sigmoid(SWIGLU_ALPHA * gate)
        return ((up + 1.0) * glu) @ w_down[e] + b_down[e]

    y_buf = lax.map(expert_block, (buf.reshape(n_blocks, MOE_BLOCK, D), block_e))
    y_sorted = y_buf.reshape(n_blocks * MOE_BLOCK, D)[dest]
    out = jnp.zeros((T, D), y_sorted.dtype).at[sorted_tok].add(y_sorted * sorted_w[:, None].astype(y_sorted.dtype))
    return out.reshape(B, S, D)


def setup_inputs(seed: int = 0) -> dict:
    key = jax.random.key(seed)
    ks = jax.random.split(key, 24)
    f32 = jnp.float32

    def nrm(k, shape, scale):
        return jax.random.normal(k, shape, f32) * scale

    u = jax.random.uniform(ks[13], (DEPTH, LRU_WIDTH), f32, 0.9, 0.999)
    a0 = u ** (1.0 / LRU_C)
    return {
        "x": nrm(ks[0], (BATCH, SEQ, D_MODEL), 1.0),
        "norm_mix_g": 1.0 + nrm(ks[1], (DEPTH, D_MODEL), 0.01),
        "w_in": nrm(ks[2], (DEPTH, D_MODEL, IN_COLS), D_MODEL ** -0.5),
        "gla_w_gate_up": nrm(ks[3], (DEPTH, GLA_GATE_RANK, GLA_KW), GLA_GATE_RANK ** -0.5),
        "gla_b_gate": nrm(ks[4], (DEPTH, GLA_KW), 0.1),
        "gla_norm_g": 1.0 + nrm(ks[5], (DEPTH, GLA_WIDTH), 0.01),
        "lru_conv_w": nrm(ks[6], (DEPTH, LRU_CONV, LRU_WIDTH), LRU_CONV ** -0.5),
        "lru_conv_b": nrm(ks[7], (DEPTH, LRU_WIDTH), 0.01),
        "lru_w_a": nrm(ks[8], (DEPTH, LRU_HEADS, LRU_BLOCK, LRU_BLOCK), LRU_BLOCK ** -0.5),
        "lru_b_a": nrm(ks[9], (DEPTH, LRU_WIDTH), 0.1),
        "lru_w_i": nrm(ks[10], (DEPTH, LRU_HEADS, LRU_BLOCK, LRU_BLOCK), LRU_BLOCK ** -0.5),
        "lru_b_i": nrm(ks[11], (DEPTH, LRU_WIDTH), 0.1),
        "lru_a_param": jnp.log(a0 / (1.0 - a0)),
        "w_out": nrm(ks[12], (DEPTH, MIX_WIDTH, D_MODEL), MIX_WIDTH ** -0.5),
        "norm_ffn_g": 1.0 + nrm(ks[14], (DEPTH, D_MODEL), 0.01),
        "w_router": nrm(ks[15], (DEPTH, D_MODEL, N_EXPERTS), D_MODEL ** -0.5),
        "b_router": nrm(ks[16], (DEPTH, N_EXPERTS), 0.01),
        "w_gate_up": nrm(ks[17], (DEPTH, N_EXPERTS, D_MODEL, 2 * D_FF), D_MODEL ** -0.5),
        "b_gate_up": nrm(ks[18], (DEPTH, N_EXPERTS, 2 * D_FF), 0.01),
        "w_down": nrm(ks[19], (DEPTH, N_EXPERTS, D_FF, D_MODEL), D_FF ** -0.5),
        "b_down": nrm(ks[20], (DEPTH, N_EXPERTS, D_MODEL), 0.01),
        "norm_final_g": 1.0 + nrm(ks[21], (D_MODEL,), 0.01),
    }


def reference(x, norm_mix_g, w_in, gla_w_gate_up, gla_b_gate, gla_norm_g, lru_conv_w, lru_conv_b,
              lru_w_a, lru_b_a, lru_w_i, lru_b_i, lru_a_param, w_out, norm_ffn_g, w_router, b_router,
              w_gate_up, b_gate_up, w_down, b_down, norm_final_g):
    B, S, _ = x.shape
    split_points = np.cumsum([GLA_KW, GLA_KW, GLA_WIDTH, GLA_WIDTH, GLA_GATE_RANK, LRU_WIDTH])
    for l in range(DEPTH):
        h = rms_norm(x, norm_mix_g[l])
        z = h @ w_in[l]
        q, k, v, r, g_low, lx, lgate = jnp.split(z, split_points, axis=-1)

        log_g = jax.nn.log_sigmoid((g_low @ gla_w_gate_up[l] + gla_b_gate[l]).astype(jnp.float32)) / GLA_GATE_NORM
        o = gla_chunked(q.reshape(B, S, GLA_HEADS, GLA_DK), k.reshape(B, S, GLA_HEADS, GLA_DK),
                        v.reshape(B, S, GLA_HEADS, GLA_DV), log_g.reshape(B, S, GLA_HEADS, GLA_DK))
        o = o * lax.rsqrt(jnp.mean(o * o, axis=-1, keepdims=True) + EPS)
        o = o.reshape(B, S, GLA_WIDTH) * gla_norm_g[l]
        gla_out = (jax.nn.silu(r.astype(jnp.float32)) * o).astype(x.dtype)

        xc = causal_depthwise_conv(lx, lru_conv_w[l], lru_conv_b[l])
        hl = rg_lru(xc, lru_w_a[l], lru_b_a[l], lru_w_i[l], lru_b_i[l], lru_a_param[l])
        lru_out = (hl * jax.nn.gelu(lgate.astype(jnp.float32))).astype(x.dtype)

        x = x + jnp.concatenate([gla_out, lru_out], axis=-1) @ w_out[l]

        hf = rms_norm(x, norm_ffn_g[l])
        x = x + moe_ffn(hf, w_router[l], b_router[l], w_gate_up[l], b_gate_up[l], w_down[l], b_down[l])
    return rms_norm(x, norm_final_g)
```

```python
import functools

import jax
import jax.numpy as jnp
from jax import lax
from jax.experimental import pallas as pl
from jax.experimental.pallas import tpu as pltpu

F32 = jnp.float32
BF16 = jnp.bfloat16
I32 = jnp.int32
U32 = jnp.uint32
HIGHEST = lax.Precision.HIGHEST

EPS = 1e-6
LANES = 128
SUBLANES = 8
VMEM_LIMIT = 56 * 1024 * 1024

GLA_HEADS = 4
GLA_DK = 128
GLA_DV = 256
GLA_GATE_RANK = 16
GLA_GATE_NORM = 16.0
GLA_CHUNK = 64
LRU_HEADS = 4
LRU_BLOCK = 256
LRU_CONV = 4
LRU_C = 8.0
N_EXPERTS = 32
TOP_K = 4
SWIGLU_LIMIT = 7.0
SWIGLU_ALPHA = 1.702

EXPERT_TILE = 512


def _params(sem, vmem=VMEM_LIMIT):
    return pltpu.CompilerParams(dimension_semantics=sem, vmem_limit_bytes=vmem)


def _rms(x, g):
    ms = jnp.mean(x * x, axis=-1, keepdims=True)
    return x * lax.rsqrt(ms + EPS) * g


def _in_proj_kernel(x_ref, g_ref, w_ref, wgl_ref, z_ref, gl_ref, h_scr):
    @pl.when(pl.program_id(1) == 0)
    def _():
        h = _rms(x_ref[...], g_ref[...]).astype(BF16)
        h_scr[...] = h
        gl_ref[...] = jnp.dot(h, wgl_ref[...], preferred_element_type=F32)

    z_ref[...] = jnp.dot(h_scr[...], w_ref[...], preferred_element_type=F32).astype(z_ref.dtype)


def in_proj(x2, g, w_main, w_glow, *, tm=1024, tn=1024):
    T, D = x2.shape
    N = w_main.shape[1]
    return pl.pallas_call(
        _in_proj_kernel,
        grid=(T // tm, N // tn),
        in_specs=[
            pl.BlockSpec((tm, D), lambda i, j: (i, 0)),
            pl.BlockSpec((1, D), lambda i, j: (0, 0)),
            pl.BlockSpec((D, tn), lambda i, j: (0, j)),
            pl.BlockSpec((D, LANES), lambda i, j: (0, 0)),
        ],
        out_specs=[
            pl.BlockSpec((tm, tn), lambda i, j: (i, j)),
            pl.BlockSpec((tm, LANES), lambda i, j: (i, 0)),
        ],
        out_shape=[jax.ShapeDtypeStruct((T, N), BF16), jax.ShapeDtypeStruct((T, LANES), F32)],
        scratch_shapes=[pltpu.VMEM((tm, D), BF16)],
        compiler_params=_params(("parallel", "arbitrary")),
        name="in_proj",
    )(x2, g, w_main, w_glow)


def _gla_kernel(q_ref, k_ref, v_ref, r_ref, gl_ref, wgu_ref, bg_ref, ng_ref, o_ref, st_scr, *, tc):
    @pl.when(pl.program_id(2) == 0)
    def _():
        st_scr[...] = jnp.zeros_like(st_scr)

    pre = jnp.dot(gl_ref[...].astype(BF16), wgu_ref[...], preferred_element_type=F32) + bg_ref[...]
    log_g = jax.nn.log_sigmoid(pre) * (1.0 / GLA_GATE_NORM)

    ri = lax.broadcasted_iota(I32, (tc, tc), 0)
    ci = lax.broadcasted_iota(I32, (tc, tc), 1)
    same = (ri // GLA_CHUNK) == (ci // GLA_CHUNK)
    cum_m = jnp.where(same & (ci <= ri), 1.0, 0.0).astype(F32)
    bcum = jnp.dot(cum_m, log_g, precision=HIGHEST, preferred_element_type=F32)

    tri = (lax.broadcasted_iota(I32, (GLA_CHUNK, GLA_CHUNK), 1)
           <= lax.broadcasted_iota(I32, (GLA_CHUNK, GLA_CHUNK), 0))
    scale = GLA_DK ** -0.5
    nt = (((1,), (1,)), ((), ()))
    tn = (((0,), (0,)), ((), ()))
    st = st_scr[...]
    outs = []
    for c in range(tc // GLA_CHUNK):
        sl = slice(c * GLA_CHUNK, (c + 1) * GLA_CHUNK)
        b = bcum[sl]
        bl = b[GLA_CHUNK - 1:GLA_CHUNK]
        qf = q_ref[sl, :].astype(F32) * scale
        kf = k_ref[sl, :].astype(F32)
        qe = (qf * jnp.exp(b)).astype(BF16)
        ke = (kf * jnp.exp(-b)).astype(BF16)
        kd = (kf * jnp.exp(bl - b)).astype(BF16)
        vb = v_ref[sl, :]
        s = lax.dot_general(qe, ke, nt, preferred_element_type=F32)
        s = jnp.where(tri, s, 0.0).astype(BF16)
        o = jnp.dot(s, vb, preferred_element_type=F32)
        o = o + lax.dot_general(qe, st.astype(BF16), nt, preferred_element_type=F32)
        upd = lax.dot_general(vb, kd, tn, preferred_element_type=F32)
        st = st * jnp.exp(bl) + upd
        outs.append(o)
    st_scr[...] = st
    o = jnp.concatenate(outs, axis=0)
    on = _rms(o, ng_ref[...])
    r = r_ref[...].astype(F32)
    o_ref[...] = (jax.nn.silu(r) * on).astype(o_ref.dtype)


def gla(z, g_low, wgu_pad, b_gate, norm_g, *, batch, seq, tc=256):
    T = z.shape[0]
    nc = seq // tc
    kw = GLA_HEADS * GLA_DK
    q0, k0 = 0, kw // GLA_DK
    v0 = (2 * kw) // GLA_DV
    r0 = v0 + GLA_HEADS
    row = lambda b, h, c: b * nc + c
    return pl.pallas_call(
        functools.partial(_gla_kernel, tc=tc),
        grid=(batch, GLA_HEADS, nc),
        in_specs=[
            pl.BlockSpec((tc, GLA_DK), lambda b, h, c: (row(b, h, c), q0 + h)),
            pl.BlockSpec((tc, GLA_DK), lambda b, h, c: (row(b, h, c), k0 + h)),
            pl.BlockSpec((tc, GLA_DV), lambda b, h, c: (row(b, h, c), v0 + h)),
            pl.BlockSpec((tc, GLA_DV), lambda b, h, c: (row(b, h, c), r0 + h)),
            pl.BlockSpec((tc, LANES), lambda b, h, c: (row(b, h, c), 0)),
            pl.BlockSpec((LANES, GLA_DK), lambda b, h, c: (0, h)),
            pl.BlockSpec((1, GLA_DK), lambda b, h, c: (0, h)),
            pl.BlockSpec((1, GLA_DV), lambda b, h, c: (0, h)),
        ],
        out_specs=pl.BlockSpec((tc, GLA_DV), lambda b, h, c: (row(b, h, c), h)),
        out_shape=jax.ShapeDtypeStruct((T, GLA_HEADS * GLA_DV), BF16),
        scratch_shapes=[pltpu.VMEM((GLA_DV, GLA_DK), F32)],
        compiler_params=_params(("parallel", "parallel", "arbitrary")),
        name="gla",
    )(z, z, z, z, g_low, wgu_pad, b_gate, norm_g)


def _lru_kernel(lx_ref, lg_ref, cw_ref, cb_ref, wa_ref, ba_ref, wi_ref, bi_ref, ap_ref, o_ref,
                tail_scr, h_scr, a_scr, u_scr, hs_scr, *, tc):
    @pl.when(pl.program_id(2) == 0)
    def _():
        tail_scr[...] = jnp.zeros_like(tail_scr)
        h_scr[...] = jnp.zeros_like(h_scr)

    x = lx_ref[...].astype(F32)
    xp = jnp.concatenate([tail_scr[...], x], axis=0)
    tail_scr[...] = x[tc - SUBLANES:, :]
    xc = cb_ref[...]
    for i in range(LRU_CONV):
        s0 = SUBLANES - (LRU_CONV - 1) + i
        xc = xc + xp[s0:s0 + tc, :] * cw_ref[i:i + 1, :]

    xb = xc.astype(BF16)
    ga = jax.nn.sigmoid(jnp.dot(xb, wa_ref[0], preferred_element_type=F32) + ba_ref[...])
    gi = jax.nn.sigmoid(jnp.dot(xb, wi_ref[0], preferred_element_type=F32) + bi_ref[...])
    log_a = -LRU_C * ga * jax.nn.softplus(-ap_ref[...])
    a = jnp.exp(log_a)
    u = xc * gi * jnp.sqrt(-jnp.tanh(log_a) * (a * a + 1.0))

    sub = lax.broadcasted_iota(I32, a.shape, 0) % SUBLANES
    for s in (1, 2, 4):
        keep = sub >= s
        a_sh = jnp.where(keep, pltpu.roll(a, s, axis=0), 1.0)
        u_sh = jnp.where(keep, pltpu.roll(u, s, axis=0), 0.0)
        u = u + a * u_sh
        a = a * a_sh
    a_scr[...] = a
    u_scr[...] = u

    def body(g, hp):
        rows = pl.ds(pl.multiple_of(g * SUBLANES, SUBLANES), SUBLANES)
        hr = a_scr[rows, :] * hp + u_scr[rows, :]
        hs_scr[rows, :] = hr
        return jnp.broadcast_to(hr[SUBLANES - 1:, :], hr.shape)

    h_scr[...] = lax.fori_loop(0, tc // SUBLANES, body, h_scr[...], unroll=8)
    gate = jax.nn.gelu(lg_ref[...].astype(F32))
    o_ref[...] = (hs_scr[...] * gate).astype(o_ref.dtype)


def lru(z, conv_w, conv_b, w_a, b_a, w_i, b_i, a_param, *, batch, seq, lx_col, tc=256):
    T = z.shape[0]
    nc = seq // tc
    W = LRU_BLOCK
    lx0 = lx_col // W
    lg0 = lx0 + LRU_HEADS
    row = lambda b, h, c: b * nc + c
    vec = pl.BlockSpec((1, W), lambda b, h, c: (0, h))
    mat = pl.BlockSpec((1, W, W), lambda b, h, c: (h, 0, 0))
    return pl.pallas_call(
        functools.partial(_lru_kernel, tc=tc),
        grid=(batch, LRU_HEADS, nc),
        in_specs=[
            pl.BlockSpec((tc, W), lambda b, h, c: (row(b, h, c), lx0 + h)),
            pl.BlockSpec((tc, W), lambda b, h, c: (row(b, h, c), lg0 + h)),
            pl.BlockSpec((LRU_CONV, W), lambda b, h, c: (0, h)),
            vec, mat, vec, mat, vec, vec,
        ],
        out_specs=pl.BlockSpec((tc, W), lambda b, h, c: (row(b, h, c), h)),
        out_shape=jax.ShapeDtypeStruct((T, LRU_HEADS * W), BF16),
        scratch_shapes=[
            pltpu.VMEM((SUBLANES, W), F32),
            pltpu.VMEM((SUBLANES, W), F32),
            pltpu.VMEM((tc, W), F32),
            pltpu.VMEM((tc, W), F32),
            pltpu.VMEM((tc, W), F32),
        ],
        compiler_params=_params(("parallel", "parallel", "arbitrary")),
        name="lru",
    )(z, z, conv_w, conv_b, w_a, b_a, w_i, b_i, a_param)


def _out_proj_kernel(go_ref, lo_ref, x_ref, wo_ref, gf_ref, wr_ref, br_ref,
                     x1_ref, hp_ref, idx_ref, gate_ref):
    half = go_ref.shape[1]
    mix = jnp.dot(go_ref[...], wo_ref[:half, :], preferred_element_type=F32)
    mix = mix + jnp.dot(lo_ref[...], wo_ref[half:, :], preferred_element_type=F32)
    x1 = x_ref[...] + mix
    x1_ref[...] = x1
    hf = _rms(x1, gf_ref[...])

    bits = pltpu.bitcast(hf.astype(BF16).astype(F32), U32)
    hd = bits.shape[1] // 2
    hp_ref[...] = (bits[:, hd:] & jnp.uint32(0xFFFF0000)) | (bits[:, :hd] >> 16)

    logits = jnp.dot(hf, wr_ref[...], precision=HIGHEST, preferred_element_type=F32) + br_ref[...]
    lane = lax.broadcasted_iota(I32, logits.shape, 1)
    lanef = lane.astype(F32)
    neg = jnp.float32(-jnp.inf)
    work = jnp.where(lane < N_EXPERTS, logits, neg)
    vals, idxs = [], []
    for _ in range(TOP_K):
        m = jnp.max(work, axis=-1, keepdims=True)
        sel = jnp.min(jnp.where(work == m, lanef, float(LANES)), axis=-1, keepdims=True)
        work = jnp.where(lanef == sel, neg, work)
        vals.append(m)
        idxs.append(sel)
    ex = [jnp.exp(v - vals[0]) for v in vals]
    den = ex[0] + ex[1] + ex[2] + ex[3]
    idx_out = jnp.zeros(logits.shape, F32)
    gate_out = jnp.zeros(logits.shape, F32)
    for k in range(TOP_K):
        idx_out = jnp.where(lane == k, idxs[k], idx_out)
        gate_out = jnp.where(lane == k, ex[k] / den, gate_out)
    idx_ref[...] = idx_out.astype(I32)
    gate_ref[...] = gate_out


def out_proj(gla_o, lru_o, x2, w_out, g_ffn, wr_pad, br_pad, *, tm=512):
    T, D = x2.shape
    half = gla_o.shape[1]
    row = lambda i: (i, 0)
    fixed = lambda i: (0, 0)
    return pl.pallas_call(
        _out_proj_kernel,
        grid=(T // tm,),
        in_specs=[
            pl.BlockSpec((tm, half), row),
            pl.BlockSpec((tm, half), row),
            pl.BlockSpec((tm, D), row),
            pl.BlockSpec((2 * half, D), fixed),
            pl.BlockSpec((1, D), fixed),
            pl.BlockSpec((D, LANES), fixed),
            pl.BlockSpec((1, LANES), fixed),
        ],
        out_specs=[
            pl.BlockSpec((tm, D), row),
            pl.BlockSpec((tm, D // 2), row),
            pl.BlockSpec((tm, LANES), row),
            pl.BlockSpec((tm, LANES), row),
        ],
        out_shape=[
            jax.ShapeDtypeStruct((T, D), F32),
            jax.ShapeDtypeStruct((T, D // 2), U32),
            jax.ShapeDtypeStruct((T, LANES), I32),
            jax.ShapeDtypeStruct((T, LANES), F32),
        ],
        compiler_params=_params(("parallel",)),
        name="out_proj",
    )(gla_o, lru_o, x2, w_out, g_ffn, wr_pad, br_pad)


def _rank_kernel(idx_ref, rank_ref, cnt_ref, carry_scr):
    @pl.when(pl.program_id(0) == 0)
    def _():
        carry_scr[...] = jnp.zeros_like(carry_scr)

    idx = idx_ref[...]
    tm = idx.shape[0]
    lane = lax.broadcasted_iota(I32, idx.shape, 1)
    hot = [idx[:, k:k + 1] == lane for k in range(TOP_K)]
    oh = jnp.zeros(idx.shape, F32)
    for k in range(TOP_K):
        oh = oh + jnp.where(hot[k], 1.0, 0.0)
    lower = (lax.broadcasted_iota(I32, (tm, tm), 1) < lax.broadcasted_iota(I32, (tm, tm), 0))
    before = jnp.dot(jnp.where(lower, 1.0, 0.0).astype(BF16), oh.astype(BF16),
                     preferred_element_type=F32) + carry_scr[0:1, :]
    out = jnp.zeros(idx.shape, F32)
    for k in range(TOP_K):
        rk = jnp.sum(jnp.where(hot[k], before, 0.0), axis=-1, keepdims=True)
        out = jnp.where(lane == k, rk, out)
    rank_ref[...] = out.astype(I32)
    carry = carry_scr[...] + jnp.sum(oh, axis=0, keepdims=True)
    carry_scr[...] = carry
    cnt_ref[...] = carry


def rank(idx_pad, *, tm=512):
    T = idx_pad.shape[0]
    return pl.pallas_call(
        _rank_kernel,
        grid=(T // tm,),
        in_specs=[pl.BlockSpec((tm, LANES), lambda i: (i, 0))],
        out_specs=[
            pl.BlockSpec((tm, LANES), lambda i: (i, 0)),
            pl.BlockSpec((SUBLANES, LANES), lambda i: (0, 0)),
        ],
        out_shape=[
            jax.ShapeDtypeStruct((T, LANES), I32),
            jax.ShapeDtypeStruct((SUBLANES, LANES), F32),
        ],
        scratch_shapes=[pltpu.VMEM((SUBLANES, LANES), F32)],
        compiler_params=_params(("arbitrary",)),
        name="rank",
    )(idx_pad)


def _scatter_kernel(pstart_ref, idx_ref, rank_ref, hp_ref, xs_in_ref, xs_ref, sem, *, tm):
    del xs_in_ref

    def row_copy(t, d):
        return pltpu.make_async_copy(hp_ref.at[pl.ds(t, 1)], xs_ref.at[pl.ds(d, 1)], sem)

    def issue(t, carry):
        for k in range(TOP_K):
            s = t * TOP_K + k
            row_copy(t, pstart_ref[idx_ref[s]] + rank_ref[s]).start()
        return carry

    lax.fori_loop(0, tm, issue, 0)

    def drain(t, carry):
        for k in range(TOP_K):
            row_copy(0, 0).wait()
        return carry

    lax.fori_loop(0, tm, drain, 0)


def scatter_rows(pstart, idx_flat, rank_flat, hp, xs_init, *, tm=512):
    T, W = hp.shape
    smem_blk = pl.BlockSpec((tm * TOP_K,), lambda i, ps: (i,), memory_space=pltpu.SMEM)
    return pl.pallas_call(
        functools.partial(_scatter_kernel, tm=tm),
        grid_spec=pltpu.PrefetchScalarGridSpec(
            num_scalar_prefetch=1,
            grid=(T // tm,),
            in_specs=[
                smem_blk, smem_blk,
                pl.BlockSpec((tm, W), lambda i, ps: (i, 0)),
                pl.BlockSpec(memory_space=pl.ANY),
            ],
            out_specs=pl.BlockSpec(memory_space=pl.ANY),
            scratch_shapes=[pltpu.SemaphoreType.DMA(())],
        ),
        out_shape=jax.ShapeDtypeStruct(xs_init.shape, xs_init.dtype),
        input_output_aliases={4: 0},
        compiler_params=_params(("arbitrary",)),
        name="scatter_rows",
    )(pstart, idx_flat, rank_flat, hp, xs_init)


def _tile_changed(te_ref, i):
    return (i == 0) | (te_ref[i] != te_ref[jnp.maximum(i - 1, 0)])


def _expert_up_kernel(te_ref, nu_ref, xs_ref, wg_ref, wu_ref, bg_ref, bu_ref, h_ref, wg_scr, wu_scr):
    i = pl.program_id(1)
    valid = i < nu_ref[0]

    @pl.when(valid & _tile_changed(te_ref, i))
    def _():
        wg_scr[...] = wg_ref[0].astype(BF16)
        wu_scr[...] = wu_ref[0].astype(BF16)

    @pl.when(valid)
    def _():
        w = xs_ref[...]
        hd = w.shape[1]
        lo = pltpu.bitcast(w << 16, F32).astype(BF16)
        hi = pltpu.bitcast(w & jnp.uint32(0xFFFF0000), F32).astype(BF16)

        def proj(w_scr, b_ref):
            acc = jnp.dot(lo, w_scr[:hd, :], preferred_element_type=F32)
            acc = acc + jnp.dot(hi, w_scr[hd:, :], preferred_element_type=F32)
            return acc + b_ref[0]

        gate = jnp.minimum(proj(wg_scr, bg_ref), SWIGLU_LIMIT)
        up = jnp.clip(proj(wu_scr, bu_ref), -SWIGLU_LIMIT, SWIGLU_LIMIT)
        glu = gate * jax.nn.sigmoid(SWIGLU_ALPHA * gate)
        h_ref[...] = ((up + 1.0) * glu).astype(h_ref.dtype)

    @pl.when(jnp.logical_not(valid))
    def _():
        h_ref[...] = jnp.zeros_like(h_ref)


def expert_up(tile_e, n_used, xs, w_gate_up, b_gate_up3, *, tm=EXPERT_TILE, tn=512):
    P, hd = xs.shape
    E, D, F2 = w_gate_up.shape
    F = F2 // 2
    nj = F // tn
    row = lambda i, nu: jnp.minimum(i, nu[0] - 1)
    return pl.pallas_call(
        _expert_up_kernel,
        grid_spec=pltpu.PrefetchScalarGridSpec(
            num_scalar_prefetch=2,
            grid=(nj, P // tm),
            in_specs=[
                pl.BlockSpec((tm, hd), lambda j, i, te, nu: (row(i, nu), 0)),
                pl.BlockSpec((1, D, tn), lambda j, i, te, nu: (te[i], 0, j)),
                pl.BlockSpec((1, D, tn), lambda j, i, te, nu: (te[i], 0, nj + j)),
                pl.BlockSpec((1, 1, tn), lambda j, i, te, nu: (te[i], 0, j)),
                pl.BlockSpec((1, 1, tn), lambda j, i, te, nu: (te[i], 0, nj + j)),
            ],
            out_specs=pl.BlockSpec((tm, tn), lambda j, i, te, nu: (i, j)),
            scratch_shapes=[pltpu.VMEM((D, tn), BF16), pltpu.VMEM((D, tn), BF16)],
        ),
        out_shape=jax.ShapeDtypeStruct((P, F), BF16),
        compiler_params=_params(("arbitrary", "arbitrary")),
        name="expert_up",
    )(tile_e, n_used, xs, w_gate_up, w_gate_up, b_gate_up3, b_gate_up3)


def _expert_dn_kernel(te_ref, nu_ref, h_ref, wd_ref, bd_ref, y_ref, wd_scr):
    i = pl.program_id(1)
    valid = i < nu_ref[0]

    @pl.when(valid & _tile_changed(te_ref, i))
    def _():
        wd_scr[...] = wd_ref[0].astype(BF16)

    @pl.when(valid)
    def _():
        y_ref[...] = jnp.dot(h_ref[...], wd_scr[...], preferred_element_type=F32) + bd_ref[0]

    @pl.when(jnp.logical_not(valid))
    def _():
        y_ref[...] = jnp.zeros_like(y_ref)


def expert_dn(tile_e, n_used, h, w_down, b_down3, *, tm=EXPERT_TILE, tn=1024):
    P, F = h.shape
    E, _, D = w_down.shape
    row = lambda i, nu: jnp.minimum(i, nu[0] - 1)
    return pl.pallas_call(
        _expert_dn_kernel,
        grid_spec=pltpu.PrefetchScalarGridSpec(
            num_scalar_prefetch=2,
            grid=(D // tn, P // tm),
            in_specs=[
                pl.BlockSpec((tm, F), lambda j, i, te, nu: (row(i, nu), 0)),
                pl.BlockSpec((1, F, tn), lambda j, i, te, nu: (te[i], 0, j)),
                pl.BlockSpec((1, 1, tn), lambda j, i, te, nu: (te[i], 0, j)),
            ],
            out_specs=pl.BlockSpec((tm, tn), lambda j, i, te, nu: (i, j)),
            scratch_shapes=[pltpu.VMEM((F, tn), BF16)],
        ),
        out_shape=jax.ShapeDtypeStruct((P, D), F32),
        compiler_params=_params(("arbitrary", "arbitrary")),
        name="expert_dn",
    )(tile_e, n_used, h, w_down, b_down3)


def _combine_kernel(pstart_ref, idx_ref, rank_ref, x1_ref, gate_ref, gfin_ref, y_ref, o_ref,
                    buf, sem, *, tm):
    def row_copy(t, k, d):
        return pltpu.make_async_copy(y_ref.at[pl.ds(d, 1)], buf.at[k, pl.ds(t, 1)], sem)

    def issue(t, carry):
        for k in range(TOP_K):
            s = t * TOP_K + k
            row_copy(t, k, pstart_ref[idx_ref[s]] + rank_ref[s]).start()
        return carry

    lax.fori_loop(0, tm, issue, 0)

    def drain(t, carry):
        for k in range(TOP_K):
            row_copy(0, 0, 0).wait()
        return carry

    lax.fori_loop(0, tm, drain, 0)

    gates = gate_ref[...]
    moe = gates[:, 0:1] * buf[0]
    for k in range(1, TOP_K):
        moe = moe + gates[:, k:k + 1] * buf[k]
    o_ref[...] = _rms(x1_ref[...] + moe, gfin_ref[...])


def combine(pstart, idx_flat, rank_flat, x1, gates, g_fin, y, *, tm=256):
    T, D = x1.shape
    smem_blk = pl.BlockSpec((tm * TOP_K,), lambda i, ps: (i,), memory_space=pltpu.SMEM)
    return pl.pallas_call(
        functools.partial(_combine_kernel, tm=tm),
        grid_spec=pltpu.PrefetchScalarGridSpec(
            num_scalar_prefetch=1,
            grid=(T // tm,),
            in_specs=[
                smem_blk, smem_blk,
                pl.BlockSpec((tm, D), lambda i, ps: (i, 0)),
                pl.BlockSpec((tm, LANES), lambda i, ps: (i, 0)),
                pl.BlockSpec((1, D), lambda i, ps: (0, 0)),
                pl.BlockSpec(memory_space=pl.ANY),
            ],
            out_specs=pl.BlockSpec((tm, D), lambda i, ps: (i, 0)),
            scratch_shapes=[pltpu.VMEM((TOP_K, tm, D), F32), pltpu.SemaphoreType.DMA(())],
        ),
        out_shape=jax.ShapeDtypeStruct((T, D), F32),
        compiler_params=_params(("arbitrary",)),
        name="combine",
    )(pstart, idx_flat, rank_flat, x1, gates, g_fin, y)


def _tile_plan(counts, n_tiles, tile):
    tiles_e = (counts + tile - 1) // tile
    ends = jnp.cumsum(tiles_e)
    pstart = (ends - tiles_e) * tile
    n_used = ends[-1]
    t = jnp.arange(n_tiles, dtype=I32)
    tile_e = jnp.searchsorted(ends, jnp.minimum(t, n_used - 1), side="right").astype(I32)
    return pstart.astype(I32), tile_e, n_used.reshape(1).astype(I32)


def _layer(x, norm_mix_g, w_in, gla_w_gate_up, gla_b_gate, gla_norm_g, lru_conv_w, lru_conv_b,
           lru_w_a, lru_b_a, lru_w_i, lru_b_i, lru_a_param, w_out, norm_ffn_g, w_router, b_router,
           w_gate_up, b_gate_up, w_down, b_down):
    B, S, D = x.shape
    T = B * S
    x2 = x.reshape(T, D)
    kw = GLA_HEADS * GLA_DK
    gw = GLA_HEADS * GLA_DV
    g0 = 2 * kw + 2 * gw

    w_main = jnp.concatenate([w_in[:, :g0], w_in[:, g0 + GLA_GATE_RANK:]], axis=1).astype(BF16)
    w_glow = jnp.pad(w_in[:, g0:g0 + GLA_GATE_RANK], ((0, 0), (0, LANES - GLA_GATE_RANK))).astype(BF16)
    z, g_low = in_proj(x2, norm_mix_g.reshape(1, D), w_main, w_glow)

    wgu_pad = jnp.pad(gla_w_gate_up, ((0, LANES - GLA_GATE_RANK), (0, 0))).astype(BF16)
    gla_o = gla(z, g_low, wgu_pad, gla_b_gate.reshape(1, kw), gla_norm_g.reshape(1, gw),
                batch=B, seq=S)
    lw = LRU_HEADS * LRU_BLOCK
    lru_o = lru(z, lru_conv_w, lru_conv_b.reshape(1, lw), lru_w_a.astype(BF16), lru_b_a.reshape(1, lw),
                lru_w_i.astype(BF16), lru_b_i.reshape(1, lw), lru_a_param.reshape(1, lw),
                batch=B, seq=S, lx_col=g0)

    wr_pad = jnp.pad(w_router, ((0, 0), (0, LANES - N_EXPERTS)))
    br_pad = jnp.pad(b_router, (0, LANES - N_EXPERTS)).reshape(1, LANES)
    x1, hp, idx_pad, gates = out_proj(gla_o, lru_o, x2, w_out.astype(BF16), norm_ffn_g.reshape(1, D),
                                      wr_pad, br_pad)

    rank_pad, cnt = rank(idx_pad)
    n_tiles = (T * TOP_K) // EXPERT_TILE + N_EXPERTS
    pstart, tile_e, n_used = _tile_plan(cnt[0, :N_EXPERTS].astype(I32), n_tiles, EXPERT_TILE)
    idx_flat = idx_pad[:, :TOP_K].reshape(-1)
    rank_flat = rank_pad[:, :TOP_K].reshape(-1)

    xs = scatter_rows(pstart, idx_flat, rank_flat, hp, jnp.zeros((n_tiles * EXPERT_TILE, D // 2), U32))
    E, _, F2 = w_gate_up.shape
    h = expert_up(tile_e, n_used, xs, w_gate_up, b_gate_up.reshape(E, 1, F2))
    y = expert_dn(tile_e, n_used, h, w_down, b_down.reshape(E, 1, D))
    return x1, gates, pstart, idx_flat, rank_flat, y


def kernel(x, norm_mix_g, w_in, gla_w_gate_up, gla_b_gate, gla_norm_g, lru_conv_w, lru_conv_b, lru_w_a,
           lru_b_a, lru_w_i, lru_b_i, lru_a_param, w_out, norm_ffn_g, w_router, b_router, w_gate_up,
           b_gate_up, w_down, b_down, norm_final_g):
    B, S, D = x.shape
    assert w_in.shape[0] == 1, "the final norm is fused into the single layer's combine stage"
    x1, gates, pstart, idx_flat, rank_flat, y = _layer(
        x, norm_mix_g[0], w_in[0], gla_w_gate_up[0], gla_b_gate[0], gla_norm_g[0], lru_conv_w[0],
        lru_conv_b[0], lru_w_a[0], lru_b_a[0], lru_w_i[0], lru_b_i[0], lru_a_param[0], w_out[0],
        norm_ffn_g[0], w_router[0], b_router[0], w_gate_up[0], b_gate_up[0], w_down[0], b_down[0])
    out = combine(pstart, idx_flat, rank_flat, x1, gates, norm_final_g.reshape(1, D), y)
    return out.reshape(B, S, D)
```

```python
import functools

import jax
import jax.numpy as jnp
from jax import lax
from jax.experimental import pallas as pl
from jax.experimental.pallas import tpu as pltpu

F32 = jnp.float32
BF16 = jnp.bfloat16
I32 = jnp.int32

EPS = 1e-6
LANES = 128
SUBLANES = 8
MXU_N = 256
VMEM_LIMIT = 56 * 1024 * 1024

GLA_HEADS = 4
GLA_DK = 128
GLA_DV = 256
GLA_GATE_RANK = 16
GLA_GATE_NORM = 16.0
GLA_CHUNK = 64
LRU_HEADS = 4
LRU_BLOCK = 256
LRU_CONV = 4
LRU_C = 8.0
N_EXPERTS = 32
TOP_K = 4
SWIGLU_LIMIT = 7.0
SWIGLU_ALPHA = 1.702

EXPERT_TILE = 512
HALF_TILE = EXPERT_TILE // 2


def _params(sem, vmem=VMEM_LIMIT):
    return pltpu.CompilerParams(dimension_semantics=sem, vmem_limit_bytes=vmem)


def _rms(x, g):
    ms = jnp.mean(x * x, axis=-1, keepdims=True)
    return x * lax.rsqrt(ms + EPS) * g


def _split3(x):
    a1 = x.astype(BF16)
    r1 = x - a1.astype(F32)
    a2 = r1.astype(BF16)
    a3 = (r1 - a2.astype(F32)).astype(BF16)
    return a1, a2, a3


def _in_proj_kernel(x_ref, g_ref, w_ref, wgl_ref, z_ref, gl_ref, h_scr):
    @pl.when(pl.program_id(1) == 0)
    def _():
        h = _rms(x_ref[...], g_ref[...]).astype(BF16)
        h_scr[...] = h
        gl_ref[...] = jnp.dot(h, wgl_ref[...], preferred_element_type=F32)

    z_ref[...] = jnp.dot(h_scr[...], w_ref[...], preferred_element_type=F32).astype(z_ref.dtype)


def in_proj(x2, g, w_main, w_glow, *, tm=1024, tn=1024):
    T, D = x2.shape
    N = w_main.shape[1]
    return pl.pallas_call(
        _in_proj_kernel,
        grid=(T // tm, N // tn),
        in_specs=[
            pl.BlockSpec((tm, D), lambda i, j: (i, 0)),
            pl.BlockSpec((1, D), lambda i, j: (0, 0)),
            pl.BlockSpec((D, tn), lambda i, j: (0, j)),
            pl.BlockSpec((D, LANES), lambda i, j: (0, 0)),
        ],
        out_specs=[
            pl.BlockSpec((tm, tn), lambda i, j: (i, j)),
            pl.BlockSpec((tm, LANES), lambda i, j: (i, 0)),
        ],
        out_shape=[jax.ShapeDtypeStruct((T, N), BF16), jax.ShapeDtypeStruct((T, LANES), F32)],
        scratch_shapes=[pltpu.VMEM((tm, D), BF16)],
        compiler_params=_params(("parallel", "arbitrary")),
        name="in_proj",
    )(x2, g, w_main, w_glow)


def _gla_kernel(q_ref, k_ref, v_ref, r_ref, gl_ref, wgu_ref, bg_ref, ng_ref, o_ref, st_scr, *, tc):
    @pl.when(pl.program_id(2) == 0)
    def _():
        st_scr[...] = jnp.zeros_like(st_scr)

    pre = jnp.dot(gl_ref[...].astype(BF16), wgu_ref[...], preferred_element_type=F32) + bg_ref[...]
    log_g = jax.nn.log_sigmoid(pre) * (1.0 / GLA_GATE_NORM)

    ri = lax.broadcasted_iota(I32, (tc, tc), 0)
    ci = lax.broadcasted_iota(I32, (tc, tc), 1)
    same = (ri // GLA_CHUNK) == (ci // GLA_CHUNK)
    cum_m = jnp.where(same & (ci <= ri), 1.0, 0.0).astype(BF16)
    bcum = None
    for piece in _split3(log_g):
        part = jnp.dot(cum_m, piece, preferred_element_type=F32)
        bcum = part if bcum is None else bcum + part

    tri = (lax.broadcasted_iota(I32, (GLA_CHUNK, GLA_CHUNK), 1)
           <= lax.broadcasted_iota(I32, (GLA_CHUNK, GLA_CHUNK), 0))
    scale = GLA_DK ** -0.5
    nt = (((1,), (1,)), ((), ()))
    tn = (((0,), (0,)), ((), ()))
    chunks = range(tc // GLA_CHUNK)
    rows = [slice(c * GLA_CHUNK, (c + 1) * GLA_CHUNK) for c in chunks]

    qe, ke, kd, dec = [], [], [], []
    for sl in rows:
        b = bcum[sl]
        bl = b[GLA_CHUNK - 1:GLA_CHUNK]
        qf = q_ref[sl, :].astype(F32) * scale
        kf = k_ref[sl, :].astype(F32)
        qe.append((qf * jnp.exp(b)).astype(BF16))
        ke.append((kf * jnp.exp(-b)).astype(BF16))
        kd.append((kf * jnp.exp(bl - b)).astype(BF16))
        dec.append(jnp.exp(bl))
    scores = [jnp.where(tri, lax.dot_general(qe[c], ke[c], nt, preferred_element_type=F32), 0.0).astype(BF16)
              for c in chunks]
    st = st_scr[...]
    outs = []
    for c in chunks:
        vb = v_ref[rows[c], :]
        o = jnp.dot(scores[c], vb, preferred_element_type=F32)
        o = o + lax.dot_general(qe[c], st.astype(BF16), nt, preferred_element_type=F32)
        st = st * dec[c] + lax.dot_general(vb, kd[c], tn, preferred_element_type=F32)
        outs.append(o)
    st_scr[...] = st
    o = jnp.concatenate(outs, axis=0)
    on = _rms(o, ng_ref[...])
    r = r_ref[...].astype(F32)
    o_ref[...] = (jax.nn.silu(r) * on).astype(o_ref.dtype)


def gla(z, g_low, wgu_pad, b_gate, norm_g, *, batch, seq, tc=256):
    T = z.shape[0]
    nc = seq // tc
    kw = GLA_HEADS * GLA_DK
    q0, k0 = 0, kw // GLA_DK
    v0 = (2 * kw) // GLA_DV
    r0 = v0 + GLA_HEADS
    row = lambda b, h, c: b * nc + c
    return pl.pallas_call(
        functools.partial(_gla_kernel, tc=tc),
        grid=(batch, GLA_HEADS, nc),
        in_specs=[
            pl.BlockSpec((tc, GLA_DK), lambda b, h, c: (row(b, h, c), q0 + h)),
            pl.BlockSpec((tc, GLA_DK), lambda b, h, c: (row(b, h, c), k0 + h)),
            pl.BlockSpec((tc, GLA_DV), lambda b, h, c: (row(b, h, c), v0 + h)),
            pl.BlockSpec((tc, GLA_DV), lambda b, h, c: (row(b, h, c), r0 + h)),
            pl.BlockSpec((tc, LANES), lambda b, h, c: (row(b, h, c), 0)),
            pl.BlockSpec((LANES, GLA_DK), lambda b, h, c: (0, h)),
            pl.BlockSpec((1, GLA_DK), lambda b, h, c: (0, h)),
            pl.BlockSpec((1, GLA_DV), lambda b, h, c: (0, h)),
        ],
        out_specs=pl.BlockSpec((tc, GLA_DV), lambda b, h, c: (row(b, h, c), h)),
        out_shape=jax.ShapeDtypeStruct((T, GLA_HEADS * GLA_DV), BF16),
        scratch_shapes=[pltpu.VMEM((GLA_DV, GLA_DK), F32)],
        compiler_params=_params(("parallel", "parallel", "arbitrary")),
        name="gla",
    )(z, z, z, z, g_low, wgu_pad, b_gate, norm_g)


def _lru_kernel(lx_ref, lg_ref, cw_ref, cb_ref, wa_ref, ba_ref, wi_ref, bi_ref, ap_ref, o_ref,
                tail_scr, h_scr, a_scr, u_scr, hs_scr, *, tc):
    @pl.when(pl.program_id(2) == 0)
    def _():
        tail_scr[...] = jnp.zeros_like(tail_scr)
        h_scr[...] = jnp.zeros_like(h_scr)

    x = lx_ref[...].astype(F32)
    xp = jnp.concatenate([tail_scr[...], x], axis=0)
    tail_scr[...] = x[tc - SUBLANES:, :]
    xc = cb_ref[...]
    for i in range(LRU_CONV):
        s0 = SUBLANES - (LRU_CONV - 1) + i
        xc = xc + xp[s0:s0 + tc, :] * cw_ref[i:i + 1, :]

    xb = xc.astype(BF16)
    ga = jax.nn.sigmoid(jnp.dot(xb, wa_ref[0], preferred_element_type=F32) + ba_ref[...])
    gi = jax.nn.sigmoid(jnp.dot(xb, wi_ref[0], preferred_element_type=F32) + bi_ref[...])
    log_a = -LRU_C * ga * jax.nn.softplus(-ap_ref[...])
    a = jnp.exp(log_a)
    u = xc * gi * jnp.sqrt(-jnp.tanh(log_a) * (a * a + 1.0))

    sub = lax.broadcasted_iota(I32, a.shape, 0) % SUBLANES
    for s in (1, 2, 4):
        keep = sub >= s
        a_sh = jnp.where(keep, pltpu.roll(a, s, axis=0), 1.0)
        u_sh = jnp.where(keep, pltpu.roll(u, s, axis=0), 0.0)
        u = u + a * u_sh
        a = a * a_sh
    a_scr[...] = a
    u_scr[...] = u

    def body(g, hp):
        rows = pl.ds(pl.multiple_of(g * SUBLANES, SUBLANES), SUBLANES)
        hr = a_scr[rows, :] * hp + u_scr[rows, :]
        hs_scr[rows, :] = hr
        return jnp.broadcast_to(hr[SUBLANES - 1:, :], hr.shape)

    h_scr[...] = lax.fori_loop(0, tc // SUBLANES, body, h_scr[...], unroll=8)
    gate = jax.nn.gelu(lg_ref[...].astype(F32))
    o_ref[...] = (hs_scr[...] * gate).astype(o_ref.dtype)


def lru(z, conv_w, conv_b, w_a, b_a, w_i, b_i, a_param, *, batch, seq, lx_col, tc=256):
    T = z.shape[0]
    nc = seq // tc
    W = LRU_BLOCK
    lx0 = lx_col // W
    lg0 = lx0 + LRU_HEADS
    row = lambda b, h, c: b * nc + c
    vec = pl.BlockSpec((1, W), lambda b, h, c: (0, h))
    mat = pl.BlockSpec((1, W, W), lambda b, h, c: (h, 0, 0))
    return pl.pallas_call(
        functools.partial(_lru_kernel, tc=tc),
        grid=(batch, LRU_HEADS, nc),
        in_specs=[
            pl.BlockSpec((tc, W), lambda b, h, c: (row(b, h, c), lx0 + h)),
            pl.BlockSpec((tc, W), lambda b, h, c: (row(b, h, c), lg0 + h)),
            pl.BlockSpec((LRU_CONV, W), lambda b, h, c: (0, h)),
            vec, mat, vec, mat, vec, vec,
        ],
        out_specs=pl.BlockSpec((tc, W), lambda b, h, c: (row(b, h, c), h)),
        out_shape=jax.ShapeDtypeStruct((T, LRU_HEADS * W), BF16),
        scratch_shapes=[
            pltpu.VMEM((SUBLANES, W), F32),
            pltpu.VMEM((SUBLANES, W), F32),
            pltpu.VMEM((tc, W), F32),
            pltpu.VMEM((tc, W), F32),
            pltpu.VMEM((tc, W), F32),
        ],
        compiler_params=_params(("parallel", "parallel", "arbitrary")),
        name="lru",
    )(z, z, conv_w, conv_b, w_a, b_a, w_i, b_i, a_param)


def _out_proj_kernel(go_ref, lo_ref, x_ref, wo_ref, gf_ref, wrh_ref, wrl_ref, br_ref,
                     x1_ref, hf_ref, idx_ref, gate_ref, cnt_ref):
    @pl.when(pl.program_id(0) == 0)
    def _():
        cnt_ref[...] = jnp.zeros_like(cnt_ref)

    half = go_ref.shape[1]
    mix = jnp.dot(go_ref[...], wo_ref[:half, :], preferred_element_type=F32)
    mix = mix + jnp.dot(lo_ref[...], wo_ref[half:, :], preferred_element_type=F32)
    x1 = x_ref[...] + mix
    x1_ref[...] = x1
    hf = _rms(x1, gf_ref[...])
    hf_ref[...] = hf
    hf_hi = hf.astype(BF16)
    hf_lo = (hf - hf_hi.astype(F32)).astype(BF16)

    logits = jnp.dot(hf_hi, wrh_ref[...], preferred_element_type=F32)
    logits = logits + jnp.dot(hf_lo, wrh_ref[...], preferred_element_type=F32)
    logits = logits + jnp.dot(hf_hi, wrl_ref[...], preferred_element_type=F32) + br_ref[...]
    lane = lax.broadcasted_iota(I32, logits.shape, 1)
    lanef = lane.astype(F32)
    neg = jnp.float32(-jnp.inf)
    work = jnp.where(lane < N_EXPERTS, logits, neg)
    vals, idxs = [], []
    hot = jnp.zeros(logits.shape, F32)
    for _ in range(TOP_K):
        m = jnp.max(work, axis=-1, keepdims=True)
        sel = jnp.min(jnp.where(work == m, lanef, float(LANES)), axis=-1, keepdims=True)
        picked = lanef == sel
        work = jnp.where(picked, neg, work)
        hot = jnp.where(picked, 1.0, hot)
        vals.append(m)
        idxs.append(sel)
    ex = [jnp.exp(v - vals[0]) for v in vals]
    den = ex[0] + ex[1] + ex[2] + ex[3]
    idx_out = jnp.zeros(logits.shape, F32)
    gate_out = jnp.zeros(logits.shape, F32)
    for k in range(TOP_K):
        idx_out = jnp.where(lane == k, idxs[k], idx_out)
        gate_out = jnp.where(lane == k, ex[k] / den, gate_out)
    idx_ref[...] = idx_out.astype(I32)
    gate_ref[...] = gate_out
    cnt_ref[...] += jnp.sum(hot, axis=0, keepdims=True)


def out_proj(gla_o, lru_o, x2, w_out, g_ffn, wr_hi, wr_lo, br_pad, *, tm=512):
    T, D = x2.shape
    half = gla_o.shape[1]
    row = lambda i: (i, 0)
    fixed = lambda i: (0, 0)
    return pl.pallas_call(
        _out_proj_kernel,
        grid=(T // tm,),
        in_specs=[
            pl.BlockSpec((tm, half), row),
            pl.BlockSpec((tm, half), row),
            pl.BlockSpec((tm, D), row),
            pl.BlockSpec((2 * half, D), fixed),
            pl.BlockSpec((1, D), fixed),
            pl.BlockSpec((D, LANES), fixed),
            pl.BlockSpec((D, LANES), fixed),
            pl.BlockSpec((1, LANES), fixed),
        ],
        out_specs=[
            pl.BlockSpec((tm, D), row),
            pl.BlockSpec((tm, D), row),
            pl.BlockSpec((tm, LANES), row),
            pl.BlockSpec((tm, LANES), row),
            pl.BlockSpec((SUBLANES, LANES), fixed),
        ],
        out_shape=[
            jax.ShapeDtypeStruct((T, D), F32),
            jax.ShapeDtypeStruct((T, D), F32),
            jax.ShapeDtypeStruct((T, LANES), I32),
            jax.ShapeDtypeStruct((T, LANES), F32),
            jax.ShapeDtypeStruct((SUBLANES, LANES), F32),
        ],
        compiler_params=_params(("arbitrary",)),
        name="out_proj",
    )(gla_o, lru_o, x2, w_out, g_ffn, wr_hi, wr_lo, br_pad)


def _rank_kernel(idx_ref, ps_ref, dest_ref, carry_scr):
    @pl.when(pl.program_id(0) == 0)
    def _():
        carry_scr[...] = jnp.zeros_like(carry_scr)

    idx = idx_ref[...]
    tm = idx.shape[0]
    lane = lax.broadcasted_iota(I32, idx.shape, 1)
    hot = [idx[:, k:k + 1] == lane for k in range(TOP_K)]
    oh = jnp.zeros(idx.shape, F32)
    for k in range(TOP_K):
        oh = oh + jnp.where(hot[k], 1.0, 0.0)
    lower = (lax.broadcasted_iota(I32, (tm, tm), 1) < lax.broadcasted_iota(I32, (tm, tm), 0))
    before = jnp.dot(jnp.where(lower, 1.0, 0.0).astype(BF16), oh.astype(BF16),
                     preferred_element_type=F32)
    base = before + (carry_scr[0:1, :] + ps_ref[...])
    out = jnp.zeros(idx.shape, F32)
    for k in range(TOP_K):
        dk = jnp.sum(jnp.where(hot[k], base, 0.0), axis=-1, keepdims=True)
        out = jnp.where(lane == k, dk, out)
    dest_ref[...] = out.astype(I32)
    carry_scr[...] += jnp.sum(oh, axis=0, keepdims=True)


def rank(idx_pad, ps_vec, *, tm=512):
    T = idx_pad.shape[0]
    return pl.pallas_call(
        _rank_kernel,
        grid=(T // tm,),
        in_specs=[
            pl.BlockSpec((tm, LANES), lambda i: (i, 0)),
            pl.BlockSpec((1, LANES), lambda i: (0, 0)),
        ],
        out_specs=pl.BlockSpec((tm, LANES), lambda i: (i, 0)),
        out_shape=jax.ShapeDtypeStruct((T, LANES), I32),
        scratch_shapes=[pltpu.VMEM((SUBLANES, LANES), F32)],
        compiler_params=_params(("arbitrary",)),
        name="rank",
    )(idx_pad, ps_vec)


def _scatter_kernel(zrow_ref, dest_ref, hp_ref, xs_ref, zero_scr, sem, zsem, *, tm):
    @pl.when(pl.program_id(0) == 0)
    def _():
        zero_scr[...] = jnp.zeros_like(zero_scr)

        def zero_copy(e):
            r0 = pl.multiple_of(zrow_ref[e], SUBLANES)
            return pltpu.make_async_copy(zero_scr, xs_ref.at[pl.ds(r0, EXPERT_TILE)], zsem)

        def zstart(e, carry):
            @pl.when(zrow_ref[e] >= 0)
            def _():
                zero_copy(e).start()
            return carry

        def zwait(e, carry):
            @pl.when(zrow_ref[e] >= 0)
            def _():
                zero_copy(e).wait()
            return carry

        lax.fori_loop(0, zrow_ref.shape[0], zstart, 0)
        lax.fori_loop(0, zrow_ref.shape[0], zwait, 0)

    def issue(t, carry):
        for k in range(TOP_K):
            d = dest_ref[t * TOP_K + k]
            pltpu.make_async_copy(hp_ref.at[pl.ds(t, 1)], xs_ref.at[pl.ds(d, 1)], sem).start()
        return carry

    lax.fori_loop(0, tm, issue, 0, unroll=8)
    for k in range(TOP_K):
        pltpu.make_async_copy(hp_ref, xs_ref.at[pl.ds(0, tm)], sem).wait()


def scatter_rows(zrow, dest_flat, hp, n_rows, *, tm=1024):
    T, W = hp.shape
    return pl.pallas_call(
        functools.partial(_scatter_kernel, tm=tm),
        grid_spec=pltpu.PrefetchScalarGridSpec(
            num_scalar_prefetch=1,
            grid=(T // tm,),
            in_specs=[
                pl.BlockSpec((tm * TOP_K,), lambda i, zr: (i,), memory_space=pltpu.SMEM),
                pl.BlockSpec((tm, W), lambda i, zr: (i, 0)),
            ],
            out_specs=pl.BlockSpec(memory_space=pl.ANY),
            scratch_shapes=[
                pltpu.VMEM((EXPERT_TILE, W), hp.dtype),
                pltpu.SemaphoreType.DMA(()),
                pltpu.SemaphoreType.DMA(()),
            ],
        ),
        out_shape=jax.ShapeDtypeStruct((n_rows, W), hp.dtype),
        compiler_params=_params(("arbitrary",)),
        name="scatter_rows",
    )(zrow, dest_flat, hp)


def _tile_changed(te_ref, i):
    return (i == 0) | (te_ref[i] != te_ref[jnp.maximum(i - 1, 0)])


def _swiglu_rows(xs_ref, wg_scr, wu_scr, bg_ref, bu_ref, h_ref, m):
    xb = xs_ref[:m, :].astype(BF16)
    tn = h_ref.shape[1]
    for n0 in range(0, tn, MXU_N):
        cols = slice(n0, n0 + MXU_N)

        def proj(w_scr, b_ref):
            return jnp.dot(xb, w_scr[:, cols], preferred_element_type=F32) + b_ref[0, :, cols]

        gate = jnp.minimum(proj(wg_scr, bg_ref), SWIGLU_LIMIT)
        up = jnp.clip(proj(wu_scr, bu_ref), -SWIGLU_LIMIT, SWIGLU_LIMIT)
        glu = gate * jax.nn.sigmoid(SWIGLU_ALPHA * gate)
        h_ref[:m, cols] = ((up + 1.0) * glu).astype(h_ref.dtype)
    if m < h_ref.shape[0]:
        h_ref[m:, :] = jnp.zeros((h_ref.shape[0] - m, tn), h_ref.dtype)


def _expert_up_kernel(te_ref, tr_ref, nu_ref, xs_ref, wg_ref, wu_ref, bg_ref, bu_ref, h_ref, wg_scr, wu_scr):
    i = pl.program_id(1)
    rows = tr_ref[i]

    @pl.when((rows > 0) & _tile_changed(te_ref, i))
    def _():
        wg_scr[...] = wg_ref[0].astype(BF16)
        wu_scr[...] = wu_ref[0].astype(BF16)

    args = (xs_ref, wg_scr, wu_scr, bg_ref, bu_ref, h_ref)

    @pl.when(rows > HALF_TILE)
    def _():
        _swiglu_rows(*args, EXPERT_TILE)

    @pl.when((rows > 0) & (rows <= HALF_TILE))
    def _():
        _swiglu_rows(*args, HALF_TILE)

    @pl.when(rows == 0)
    def _():
        h_ref[...] = jnp.zeros_like(h_ref)


def expert_up(tile_e, tile_rows, n_used, xs, w_gate_up, b_gate_up3, *, tm=EXPERT_TILE, tn=512):
    P, hd = xs.shape
    E, D, F2 = w_gate_up.shape
    F = F2 // 2
    nj = F // tn
    row = lambda i, nu: jnp.minimum(i, nu[0] - 1)
    return pl.pallas_call(
        _expert_up_kernel,
        grid_spec=pltpu.PrefetchScalarGridSpec(
            num_scalar_prefetch=3,
            grid=(nj, P // tm),
            in_specs=[
                pl.BlockSpec((tm, hd), lambda j, i, te, tr, nu: (row(i, nu), 0)),
                pl.BlockSpec((1, D, tn), lambda j, i, te, tr, nu: (te[i], 0, j)),
                pl.BlockSpec((1, D, tn), lambda j, i, te, tr, nu: (te[i], 0, nj + j)),
                pl.BlockSpec((1, 1, tn), lambda j, i, te, tr, nu: (te[i], 0, j)),
                pl.BlockSpec((1, 1, tn), lambda j, i, te, tr, nu: (te[i], 0, nj + j)),
            ],
            out_specs=pl.BlockSpec((tm, tn), lambda j, i, te, tr, nu: (i, j)),
            scratch_shapes=[pltpu.VMEM((D, tn), BF16), pltpu.VMEM((D, tn), BF16)],
        ),
        out_shape=jax.ShapeDtypeStruct((P, F), BF16),
        compiler_params=_params(("arbitrary", "arbitrary")),
        name="expert_up",
    )(tile_e, tile_rows, n_used, xs, w_gate_up, w_gate_up, b_gate_up3, b_gate_up3)


def _down_rows(h_ref, wd_scr, bd_ref, y_ref, m):
    y_ref[:m, :] = jnp.dot(h_ref[:m, :], wd_scr[...], preferred_element_type=F32) + bd_ref[0]
    if m < y_ref.shape[0]:
        y_ref[m:, :] = jnp.zeros((y_ref.shape[0] - m, y_ref.shape[1]), y_ref.dtype)


def _expert_dn_kernel(te_ref, tr_ref, nu_ref, h_ref, wd_ref, bd_ref, y_ref, wd_scr):
    i = pl.program_id(1)
    rows = tr_ref[i]

    @pl.when((rows > 0) & _tile_changed(te_ref, i))
    def _():
        wd_scr[...] = wd_ref[0].astype(BF16)

    @pl.when(rows > HALF_TILE)
    def _():
        _down_rows(h_ref, wd_scr, bd_ref, y_ref, EXPERT_TILE)

    @pl.when((rows > 0) & (rows <= HALF_TILE))
    def _():
        _down_rows(h_ref, wd_scr, bd_ref, y_ref, HALF_TILE)

    @pl.when(rows == 0)
    def _():
        y_ref[...] = jnp.zeros_like(y_ref)


def expert_dn(tile_e, tile_rows, n_used, h, w_down, b_down3, *, tm=EXPERT_TILE, tn=1024):
    P, F = h.shape
    E, _, D = w_down.shape
    row = lambda i, nu: jnp.minimum(i, nu[0] - 1)
    return pl.pallas_call(
        _expert_dn_kernel,
        grid_spec=pltpu.PrefetchScalarGridSpec(
            num_scalar_prefetch=3,
            grid=(D // tn, P // tm),
            in_specs=[
                pl.BlockSpec((tm, F), lambda j, i, te, tr, nu: (row(i, nu), 0)),
                pl.BlockSpec((1, F, tn), lambda j, i, te, tr, nu: (te[i], 0, j)),
                pl.BlockSpec((1, 1, tn), lambda j, i, te, tr, nu: (te[i], 0, j)),
            ],
            out_specs=pl.BlockSpec((tm, tn), lambda j, i, te, tr, nu: (i, j)),
            scratch_shapes=[pltpu.VMEM((F, tn), BF16)],
        ),
        out_shape=jax.ShapeDtypeStruct((P, D), F32),
        compiler_params=_params(("arbitrary", "arbitrary")),
        name="expert_dn",
    )(tile_e, tile_rows, n_used, h, w_down, b_down3)


def _combine_kernel(dcur_ref, dnxt_ref, x1_ref, gate_ref, gfin_ref, y_ref, o_ref, buf, sems, *, tm):
    i = pl.program_id(0)
    n = pl.num_programs(0)
    slot = i % 2

    def gather(dest_ref, s):
        def issue(t, carry):
            for k in range(TOP_K):
                d = dest_ref[t * TOP_K + k]
                pltpu.make_async_copy(y_ref.at[pl.ds(d, 1)], buf.at[s, k, pl.ds(t, 1)], sems.at[s]).start()
            return carry

        lax.fori_loop(0, tm, issue, 0, unroll=8)

    @pl.when(i == 0)
    def _():
        gather(dcur_ref, 0)

    @pl.when(i + 1 < n)
    def _():
        gather(dnxt_ref, 1 - slot)

    for k in range(TOP_K):
        pltpu.make_async_copy(y_ref.at[pl.ds(0, tm)], buf.at[slot, k], sems.at[slot]).wait()

    gates = gate_ref[...]
    moe = gates[:, 0:1] * buf[slot, 0]
    for k in range(1, TOP_K):
        moe = moe + gates[:, k:k + 1] * buf[slot, k]
    o_ref[...] = _rms(x1_ref[...] + moe, gfin_ref[...])


def combine(dest_flat, x1, gates, g_fin, y, *, tm=256):
    T, D = x1.shape
    n = T // tm
    blk = tm * TOP_K
    return pl.pallas_call(
        functools.partial(_combine_kernel, tm=tm),
        grid=(n,),
        in_specs=[
            pl.BlockSpec((blk,), lambda i: (i,), memory_space=pltpu.SMEM),
            pl.BlockSpec((blk,), lambda i: (jnp.minimum(i + 1, n - 1),), memory_space=pltpu.SMEM),
            pl.BlockSpec((tm, D), lambda i: (i, 0)),
            pl.BlockSpec((tm, LANES), lambda i: (i, 0)),
            pl.BlockSpec((1, D), lambda i: (0, 0)),
            pl.BlockSpec(memory_space=pl.ANY),
        ],
        out_specs=pl.BlockSpec((tm, D), lambda i: (i, 0)),
        out_shape=jax.ShapeDtypeStruct((T, D), F32),
        scratch_shapes=[pltpu.VMEM((2, TOP_K, tm, D), F32), pltpu.SemaphoreType.DMA((2,))],
        compiler_params=_params(("arbitrary",)),
        name="combine",
    )(dest_flat, dest_flat, x1, gates, g_fin, y)


def _tile_plan(counts, n_tiles, tile):
    tiles_e = (counts + tile - 1) // tile
    ends = jnp.cumsum(tiles_e)
    starts = ends - tiles_e
    pstart = starts * tile
    n_used = ends[-1]
    t = jnp.arange(n_tiles, dtype=I32)
    tc = jnp.minimum(t, n_used - 1)
    tile_e = jnp.sum((tc[:, None] >= ends[None, :]).astype(I32), axis=1)
    left = counts[tile_e] - (tc - starts[tile_e]) * tile
    tile_rows = jnp.where(t < n_used, jnp.clip(left, 0, tile), 0)
    spare = n_used + jnp.arange(counts.shape[0], dtype=I32)
    zrow = jnp.concatenate([jnp.where(tiles_e > 0, (ends - 1) * tile, -1),
                            jnp.where(spare < n_tiles, spare * tile, -1)])
    return (pstart.astype(I32), tile_e.astype(I32), tile_rows.astype(I32),
            n_used.reshape(1).astype(I32), zrow.astype(I32))


def kernel(x, norm_mix_g, w_in, gla_w_gate_up, gla_b_gate, gla_norm_g, lru_conv_w, lru_conv_b, lru_w_a,
           lru_b_a, lru_w_i, lru_b_i, lru_a_param, w_out, norm_ffn_g, w_router, b_router, w_gate_up,
           b_gate_up, w_down, b_down, norm_final_g):
    B, S, D = x.shape
    assert w_in.shape[0] == 1, "the final norm is fused into the single layer's combine stage"
    T = B * S
    x2 = x.reshape(T, D)
    kw = GLA_HEADS * GLA_DK
    gw = GLA_HEADS * GLA_DV
    lw = LRU_HEADS * LRU_BLOCK
    g0 = 2 * kw + 2 * gw

    w_in0 = w_in[0]
    w_main = jnp.concatenate([w_in0[:, :g0], w_in0[:, g0 + GLA_GATE_RANK:]], axis=1).astype(BF16)
    w_glow = jnp.pad(w_in0[:, g0:g0 + GLA_GATE_RANK], ((0, 0), (0, LANES - GLA_GATE_RANK))).astype(BF16)
    z, g_low = in_proj(x2, norm_mix_g[0].reshape(1, D), w_main, w_glow)

    wgu_pad = jnp.pad(gla_w_gate_up[0], ((0, LANES - GLA_GATE_RANK), (0, 0))).astype(BF16)
    gla_o = gla(z, g_low, wgu_pad, gla_b_gate[0].reshape(1, kw), gla_norm_g[0].reshape(1, gw),
                batch=B, seq=S)
    lru_o = lru(z, lru_conv_w[0], lru_conv_b[0].reshape(1, lw), lru_w_a[0].astype(BF16),
                lru_b_a[0].reshape(1, lw), lru_w_i[0].astype(BF16), lru_b_i[0].reshape(1, lw),
                lru_a_param[0].reshape(1, lw), batch=B, seq=S, lx_col=g0)

    wr_pad = jnp.pad(w_router[0], ((0, 0), (0, LANES - N_EXPERTS)))
    wr_hi = wr_pad.astype(BF16)
    wr_lo = (wr_pad - wr_hi.astype(F32)).astype(BF16)
    br_pad = jnp.pad(b_router[0], (0, LANES - N_EXPERTS)).reshape(1, LANES)
    x1, hf, idx_pad, gates, cnt = out_proj(gla_o, lru_o, x2, w_out[0].astype(BF16),
                                           norm_ffn_g[0].reshape(1, D), wr_hi, wr_lo, br_pad)

    n_tiles = (T * TOP_K) // EXPERT_TILE + N_EXPERTS
    pstart, tile_e, tile_rows, n_used, zrow = _tile_plan(cnt[0, :N_EXPERTS].astype(I32), n_tiles, EXPERT_TILE)
    ps_vec = jnp.pad(pstart.astype(F32), (0, LANES - N_EXPERTS)).reshape(1, LANES)
    dest_pad = rank(idx_pad, ps_vec)
    dest_flat = dest_pad[:, :TOP_K].reshape(-1)

    xs = scatter_rows(zrow, dest_flat, hf, n_tiles * EXPERT_TILE)
    E, _, F2 = w_gate_up.shape[1:]
    h = expert_up(tile_e, tile_rows, n_used, xs, w_gate_up[0], b_gate_up[0].reshape(E, 1, F2))
    y = expert_dn(tile_e, tile_rows, n_used, h, w_down[0], b_down[0].reshape(E, 1, D))
    out = combine(dest_flat, x1, gates, norm_final_g.reshape(1, D), y)
    return out.reshape(B, S, D)
```

```python
import functools

import jax
import jax.numpy as jnp
from jax import lax
from jax.experimental import pallas as pl
from jax.experimental.pallas import tpu as pltpu

F32 = jnp.float32
BF16 = jnp.bfloat16
I32 = jnp.int32

EPS = 1e-6
LANES = 128
SUBLANES = 8
MXU_N = 256
VMEM_LIMIT = 56 * 1024 * 1024

GLA_HEADS = 4
GLA_DK = 128
GLA_DV = 256
GLA_GATE_RANK = 16
GLA_GATE_NORM = 16.0
GLA_CHUNK = 64
LRU_HEADS = 4
LRU_BLOCK = 256
LRU_CONV = 4
LRU_C = 8.0
N_EXPERTS = 32
TOP_K = 4
SWIGLU_LIMIT = 7.0
SWIGLU_ALPHA = 1.702

EXPERT_TILE = 512
HALF_TILE = EXPERT_TILE // 2
CAST_ROWS = 32


def _params(sem, vmem=VMEM_LIMIT):
    return pltpu.CompilerParams(dimension_semantics=sem, vmem_limit_bytes=vmem)


def _rms(x, g):
    ms = jnp.mean(x * x, axis=-1, keepdims=True)
    return x * lax.rsqrt(ms + EPS) * g


def _split3(x):
    a1 = x.astype(BF16)
    r1 = x - a1.astype(F32)
    a2 = r1.astype(BF16)
    a3 = (r1 - a2.astype(F32)).astype(BF16)
    return a1, a2, a3


def _in_proj_kernel(x_ref, g_ref, w_ref, wgl_ref, z_ref, gl_ref, h_scr):
    @pl.when(pl.program_id(1) == 0)
    def _():
        h = _rms(x_ref[...], g_ref[...]).astype(BF16)
        h_scr[...] = h
        gl_ref[...] = jnp.dot(h, wgl_ref[...], preferred_element_type=F32)

    z_ref[...] = jnp.dot(h_scr[...], w_ref[...], preferred_element_type=F32).astype(z_ref.dtype)


def in_proj(x2, g, w_main, w_glow, *, tm=1024, tn=1024):
    T, D = x2.shape
    N = w_main.shape[1]
    return pl.pallas_call(
        _in_proj_kernel,
        grid=(T // tm, N // tn),
        in_specs=[
            pl.BlockSpec((tm, D), lambda i, j: (i, 0)),
            pl.BlockSpec((1, D), lambda i, j: (0, 0)),
            pl.BlockSpec((D, tn), lambda i, j: (0, j)),
            pl.BlockSpec((D, LANES), lambda i, j: (0, 0)),
        ],
        out_specs=[
            pl.BlockSpec((tm, tn), lambda i, j: (i, j)),
            pl.BlockSpec((tm, LANES), lambda i, j: (i, 0)),
        ],
        out_shape=[jax.ShapeDtypeStruct((T, N), BF16), jax.ShapeDtypeStruct((T, LANES), F32)],
        scratch_shapes=[pltpu.VMEM((tm, D), BF16)],
        compiler_params=_params(("parallel", "arbitrary")),
        name="in_proj",
    )(x2, g, w_main, w_glow)


def _gla_kernel(q_ref, k_ref, v_ref, r_ref, gl_ref, wgu_ref, bg_ref, ng_ref, o_ref, st_scr, *, tc):
    @pl.when(pl.program_id(2) == 0)
    def _():
        st_scr[...] = jnp.zeros_like(st_scr)

    pre = jnp.dot(gl_ref[...].astype(BF16), wgu_ref[...], preferred_element_type=F32) + bg_ref[...]
    log_g = jax.nn.log_sigmoid(pre) * (1.0 / GLA_GATE_NORM)

    ri = lax.broadcasted_iota(I32, (tc, tc), 0)
    ci = lax.broadcasted_iota(I32, (tc, tc), 1)
    same = (ri // GLA_CHUNK) == (ci // GLA_CHUNK)
    cum_m = jnp.where(same & (ci <= ri), 1.0, 0.0).astype(BF16)
    bcum = None
    for piece in _split3(log_g):
        part = jnp.dot(cum_m, piece, preferred_element_type=F32)
        bcum = part if bcum is None else bcum + part

    tri = (lax.broadcasted_iota(I32, (GLA_CHUNK, GLA_CHUNK), 1)
           <= lax.broadcasted_iota(I32, (GLA_CHUNK, GLA_CHUNK), 0))
    scale = GLA_DK ** -0.5
    nt = (((1,), (1,)), ((), ()))
    tn = (((0,), (0,)), ((), ()))
    chunks = range(tc // GLA_CHUNK)
    rows = [slice(c * GLA_CHUNK, (c + 1) * GLA_CHUNK) for c in chunks]

    qe, ke, kd, dec = [], [], [], []
    for sl in rows:
        b = bcum[sl]
        bl = b[GLA_CHUNK - 1:GLA_CHUNK]
        qf = q_ref[sl, :].astype(F32) * scale
        kf = k_ref[sl, :].astype(F32)
        qe.append((qf * jnp.exp(b)).astype(BF16))
        ke.append((kf * jnp.exp(-b)).astype(BF16))
        kd.append((kf * jnp.exp(bl - b)).astype(BF16))
        dec.append(jnp.exp(bl))
    scores = [jnp.where(tri, lax.dot_general(qe[c], ke[c], nt, preferred_element_type=F32), 0.0).astype(BF16)
              for c in chunks]
    st = st_scr[...]
    outs = []
    for c in chunks:
        vb = v_ref[rows[c], :]
        o = jnp.dot(scores[c], vb, preferred_element_type=F32)
        o = o + lax.dot_general(qe[c], st.astype(BF16), nt, preferred_element_type=F32)
        st = st * dec[c] + lax.dot_general(vb, kd[c], tn, preferred_element_type=F32)
        outs.append(o)
    st_scr[...] = st
    o = jnp.concatenate(outs, axis=0)
    on = _rms(o, ng_ref[...])
    r = r_ref[...].astype(F32)
    o_ref[...] = (jax.nn.silu(r) * on).astype(o_ref.dtype)


def gla(z, g_low, wgu_pad, b_gate, norm_g, *, batch, seq, tc=256):
    T = z.shape[0]
    nc = seq // tc
    kw = GLA_HEADS * GLA_DK
    q0, k0 = 0, kw // GLA_DK
    v0 = (2 * kw) // GLA_DV
    r0 = v0 + GLA_HEADS
    row = lambda b, h, c: b * nc + c
    return pl.pallas_call(
        functools.partial(_gla_kernel, tc=tc),
        grid=(batch, GLA_HEADS, nc),
        in_specs=[
            pl.BlockSpec((tc, GLA_DK), lambda b, h, c: (row(b, h, c), q0 + h)),
            pl.BlockSpec((tc, GLA_DK), lambda b, h, c: (row(b, h, c), k0 + h)),
            pl.BlockSpec((tc, GLA_DV), lambda b, h, c: (row(b, h, c), v0 + h)),
            pl.BlockSpec((tc, GLA_DV), lambda b, h, c: (row(b, h, c), r0 + h)),
            pl.BlockSpec((tc, LANES), lambda b, h, c: (row(b, h, c), 0)),
            pl.BlockSpec((LANES, GLA_DK), lambda b, h, c: (0, h)),
            pl.BlockSpec((1, GLA_DK), lambda b, h, c: (0, h)),
            pl.BlockSpec((1, GLA_DV), lambda b, h, c: (0, h)),
        ],
        out_specs=pl.BlockSpec((tc, GLA_DV), lambda b, h, c: (row(b, h, c), h)),
        out_shape=jax.ShapeDtypeStruct((T, GLA_HEADS * GLA_DV), BF16),
        scratch_shapes=[pltpu.VMEM((GLA_DV, GLA_DK), F32)],
        compiler_params=_params(("parallel", "parallel", "arbitrary")),
        name="gla",
    )(z, z, z, z, g_low, wgu_pad, b_gate, norm_g)


def _lru_kernel(lx_ref, lg_ref, cw_ref, cb_ref, wa_ref, ba_ref, wi_ref, bi_ref, ap_ref, o_ref,
                tail_scr, h_scr, a_scr, u_scr, hs_scr, *, tc):
    @pl.when(pl.program_id(2) == 0)
    def _():
        tail_scr[...] = jnp.zeros_like(tail_scr)
        h_scr[...] = jnp.zeros_like(h_scr)

    x = lx_ref[...].astype(F32)
    xp = jnp.concatenate([tail_scr[...], x], axis=0)
    tail_scr[...] = x[tc - SUBLANES:, :]
    xc = cb_ref[...]
    for i in range(LRU_CONV):
        s0 = SUBLANES - (LRU_CONV - 1) + i
        xc = xc + xp[s0:s0 + tc, :] * cw_ref[i:i + 1, :]

    xb = xc.astype(BF16)
    ga = jax.nn.sigmoid(jnp.dot(xb, wa_ref[0], preferred_element_type=F32) + ba_ref[...])
    gi = jax.nn.sigmoid(jnp.dot(xb, wi_ref[0], preferred_element_type=F32) + bi_ref[...])
    log_a = -LRU_C * ga * jax.nn.softplus(-ap_ref[...])
    a = jnp.exp(log_a)
    u = xc * gi * jnp.sqrt(-jnp.tanh(log_a) * (a * a + 1.0))

    sub = lax.broadcasted_iota(I32, a.shape, 0) % SUBLANES
    for s in (1, 2, 4):
        keep = sub >= s
        a_sh = jnp.where(keep, pltpu.roll(a, s, axis=0), 1.0)
        u_sh = jnp.where(keep, pltpu.roll(u, s, axis=0), 0.0)
        u = u + a * u_sh
        a = a * a_sh
    a_scr[...] = a
    u_scr[...] = u

    def body(g, hp):
        rows = pl.ds(pl.multiple_of(g * SUBLANES, SUBLANES), SUBLANES)
        hr = a_scr[rows, :] * hp + u_scr[rows, :]
        hs_scr[rows, :] = hr
        return jnp.broadcast_to(hr[SUBLANES - 1:, :], hr.shape)

    h_scr[...] = lax.fori_loop(0, tc // SUBLANES, body, h_scr[...], unroll=8)
    gate = jax.nn.gelu(lg_ref[...].astype(F32))
    o_ref[...] = (hs_scr[...] * gate).astype(o_ref.dtype)


def lru(z, conv_w, conv_b, w_a, b_a, w_i, b_i, a_param, *, batch, seq, lx_col, tc=256):
    T = z.shape[0]
    nc = seq // tc
    W = LRU_BLOCK
    lx0 = lx_col // W
    lg0 = lx0 + LRU_HEADS
    row = lambda b, h, c: b * nc + c
    vec = pl.BlockSpec((1, W), lambda b, h, c: (0, h))
    mat = pl.BlockSpec((1, W, W), lambda b, h, c: (h, 0, 0))
    return pl.pallas_call(
        functools.partial(_lru_kernel, tc=tc),
        grid=(batch, LRU_HEADS, nc),
        in_specs=[
            pl.BlockSpec((tc, W), lambda b, h, c: (row(b, h, c), lx0 + h)),
            pl.BlockSpec((tc, W), lambda b, h, c: (row(b, h, c), lg0 + h)),
            pl.BlockSpec((LRU_CONV, W), lambda b, h, c: (0, h)),
            vec, mat, vec, mat, vec, vec,
        ],
        out_specs=pl.BlockSpec((tc, W), lambda b, h, c: (row(b, h, c), h)),
        out_shape=jax.ShapeDtypeStruct((T, LRU_HEADS * W), BF16),
        scratch_shapes=[
            pltpu.VMEM((SUBLANES, W), F32),
            pltpu.VMEM((SUBLANES, W), F32),
            pltpu.VMEM((tc, W), F32),
            pltpu.VMEM((tc, W), F32),
            pltpu.VMEM((tc, W), F32),
        ],
        compiler_params=_params(("parallel", "parallel", "arbitrary")),
        name="lru",
    )(z, z, conv_w, conv_b, w_a, b_a, w_i, b_i, a_param)


def _out_proj_kernel(go_ref, lo_ref, x_ref, wo_ref, gf_ref, wrh_ref, wrl_ref, br_ref,
                     x1_ref, hf_ref, idx_ref, gate_ref, cnt_ref):
    @pl.when(pl.program_id(0) == 0)
    def _():
        cnt_ref[...] = jnp.zeros_like(cnt_ref)

    half = go_ref.shape[1]
    mix = jnp.dot(go_ref[...], wo_ref[:half, :], preferred_element_type=F32)
    mix = mix + jnp.dot(lo_ref[...], wo_ref[half:, :], preferred_element_type=F32)
    x1 = x_ref[...] + mix
    x1_ref[...] = x1
    hf = _rms(x1, gf_ref[...])
    hf_ref[...] = hf
    hf_hi = hf.astype(BF16)
    hf_lo = (hf - hf_hi.astype(F32)).astype(BF16)

    logits = jnp.dot(hf_hi, wrh_ref[...], preferred_element_type=F32)
    logits = logits + jnp.dot(hf_lo, wrh_ref[...], preferred_element_type=F32)
    logits = logits + jnp.dot(hf_hi, wrl_ref[...], preferred_element_type=F32) + br_ref[...]
    lane = lax.broadcasted_iota(I32, logits.shape, 1)
    lanef = lane.astype(F32)
    neg = jnp.float32(-jnp.inf)
    work = jnp.where(lane < N_EXPERTS, logits, neg)
    vals, idxs = [], []
    hot = jnp.zeros(logits.shape, F32)
    for _ in range(TOP_K):
        m = jnp.max(work, axis=-1, keepdims=True)
        sel = jnp.min(jnp.where(work == m, lanef, float(LANES)), axis=-1, keepdims=True)
        picked = lanef == sel
        work = jnp.where(picked, neg, work)
        hot = jnp.where(picked, 1.0, hot)
        vals.append(m)
        idxs.append(sel)
    ex = [jnp.exp(v - vals[0]) for v in vals]
    den = ex[0] + ex[1] + ex[2] + ex[3]
    idx_out = jnp.zeros(logits.shape, F32)
    gate_out = jnp.zeros(logits.shape, F32)
    for k in range(TOP_K):
        idx_out = jnp.where(lane == k, idxs[k], idx_out)
        gate_out = jnp.where(lane == k, ex[k] / den, gate_out)
    idx_ref[...] = idx_out.astype(I32)
    gate_ref[...] = gate_out
    cnt_ref[...] += jnp.sum(hot, axis=0, keepdims=True)


def out_proj(gla_o, lru_o, x2, w_out, g_ffn, wr_hi, wr_lo, br_pad, *, tm=512):
    T, D = x2.shape
    half = gla_o.shape[1]
    row = lambda i: (i, 0)
    fixed = lambda i: (0, 0)
    return pl.pallas_call(
        _out_proj_kernel,
        grid=(T // tm,),
        in_specs=[
            pl.BlockSpec((tm, half), row),
            pl.BlockSpec((tm, half), row),
            pl.BlockSpec((tm, D), row),
            pl.BlockSpec((2 * half, D), fixed),
            pl.BlockSpec((1, D), fixed),
            pl.BlockSpec((D, LANES), fixed),
            pl.BlockSpec((D, LANES), fixed),
            pl.BlockSpec((1, LANES), fixed),
        ],
        out_specs=[
            pl.BlockSpec((tm, D), row),
            pl.BlockSpec((tm, D), row),
            pl.BlockSpec((tm, LANES), row),
            pl.BlockSpec((tm, LANES), row),
            pl.BlockSpec((SUBLANES, LANES), fixed),
        ],
        out_shape=[
            jax.ShapeDtypeStruct((T, D), F32),
            jax.ShapeDtypeStruct((T, D), F32),
            jax.ShapeDtypeStruct((T, LANES), I32),
            jax.ShapeDtypeStruct((T, LANES), F32),
            jax.ShapeDtypeStruct((SUBLANES, LANES), F32),
        ],
        compiler_params=_params(("arbitrary",)),
        name="out_proj",
    )(gla_o, lru_o, x2, w_out, g_ffn, wr_hi, wr_lo, br_pad)


def _rank_kernel(idx_ref, ps_ref, dest_ref, carry_scr):
    @pl.when(pl.program_id(0) == 0)
    def _():
        carry_scr[...] = jnp.zeros_like(carry_scr)

    idx = idx_ref[...]
    tm = idx.shape[0]
    lane = lax.broadcasted_iota(I32, idx.shape, 1)
    hot = [idx[:, k:k + 1] == lane for k in range(TOP_K)]
    oh = jnp.zeros(idx.shape, F32)
    for k in range(TOP_K):
        oh = oh + jnp.where(hot[k], 1.0, 0.0)
    lower = (lax.broadcasted_iota(I32, (tm, tm), 1) < lax.broadcasted_iota(I32, (tm, tm), 0))
    before = jnp.dot(jnp.where(lower, 1.0, 0.0).astype(BF16), oh.astype(BF16),
                     preferred_element_type=F32)
    base = before + (carry_scr[0:1, :] + ps_ref[...])
    out = jnp.zeros(idx.shape, F32)
    for k in range(TOP_K):
        dk = jnp.sum(jnp.where(hot[k], base, 0.0), axis=-1, keepdims=True)
        out = jnp.where(lane == k, dk, out)
    dest_ref[...] = out.astype(I32)
    carry_scr[...] += jnp.sum(oh, axis=0, keepdims=True)


def rank(idx_pad, ps_vec, *, tm=512):
    T = idx_pad.shape[0]
    return pl.pallas_call(
        _rank_kernel,
        grid=(T // tm,),
        in_specs=[
            pl.BlockSpec((tm, LANES), lambda i: (i, 0)),
            pl.BlockSpec((1, LANES), lambda i: (0, 0)),
        ],
        out_specs=pl.BlockSpec((tm, LANES), lambda i: (i, 0)),
        out_shape=jax.ShapeDtypeStruct((T, LANES), I32),
        scratch_shapes=[pltpu.VMEM((SUBLANES, LANES), F32)],
        compiler_params=_params(("arbitrary",)),
        name="rank",
    )(idx_pad, ps_vec)


def _scatter_kernel(zrow_ref, dest_ref, hp_ref, xs_ref, zero_scr, sem, zsem, *, tm):
    @pl.when(pl.program_id(0) == 0)
    def _():
        zero_scr[...] = jnp.zeros_like(zero_scr)

        def zero_copy(e):
            r0 = pl.multiple_of(zrow_ref[e], SUBLANES)
            return pltpu.make_async_copy(zero_scr, xs_ref.at[pl.ds(r0, EXPERT_TILE)], zsem)

        def zstart(e, carry):
            @pl.when(zrow_ref[e] >= 0)
            def _():
                zero_copy(e).start()
            return carry

        def zwait(e, carry):
            @pl.when(zrow_ref[e] >= 0)
            def _():
                zero_copy(e).wait()
            return carry

        lax.fori_loop(0, zrow_ref.shape[0], zstart, 0)
        lax.fori_loop(0, zrow_ref.shape[0], zwait, 0)

    def issue(t, carry):
        for k in range(TOP_K):
            d = dest_ref[t * TOP_K + k]
            pltpu.make_async_copy(hp_ref.at[pl.ds(t, 1)], xs_ref.at[pl.ds(d, 1)], sem).start()
        return carry

    lax.fori_loop(0, tm, issue, 0, unroll=8)
    for k in range(TOP_K):
        pltpu.make_async_copy(hp_ref, xs_ref.at[pl.ds(0, tm)], sem).wait()


def scatter_rows(zrow, dest_flat, hp, n_rows, *, tm=1024):
    T, W = hp.shape
    return pl.pallas_call(
        functools.partial(_scatter_kernel, tm=tm),
        grid_spec=pltpu.PrefetchScalarGridSpec(
            num_scalar_prefetch=1,
            grid=(T // tm,),
            in_specs=[
                pl.BlockSpec((tm * TOP_K,), lambda i, zr: (i,), memory_space=pltpu.SMEM),
                pl.BlockSpec((tm, W), lambda i, zr: (i, 0)),
            ],
            out_specs=pl.BlockSpec(memory_space=pl.ANY),
            scratch_shapes=[
                pltpu.VMEM((EXPERT_TILE, W), hp.dtype),
                pltpu.SemaphoreType.DMA(()),
                pltpu.SemaphoreType.DMA(()),
            ],
        ),
        out_shape=jax.ShapeDtypeStruct((n_rows, W), hp.dtype),
        compiler_params=_params(("arbitrary",)),
        name="scatter_rows",
    )(zrow, dest_flat, hp)


def _tile_changed(te_ref, i):
    return (i == 0) | (te_ref[i] != te_ref[jnp.maximum(i - 1, 0)])


def _stream_expert_weights(te_ref, tr_ref, ne_ref, fetch, cast, nj):
    j = pl.program_id(0)
    i = pl.program_id(1)

    @pl.when((tr_ref[i] > 0) & _tile_changed(te_ref, i))
    def _():
        @pl.when((j == 0) & (i == 0))
        def _():
            for c in fetch(te_ref[0], 0):
                c.start()

        for c in fetch(te_ref[i], j):
            c.wait()
        cast()
        wrap = ne_ref[i] < 0
        e_next = jnp.where(wrap, te_ref[0], ne_ref[i])
        j_next = jnp.where(wrap, j + 1, j)

        @pl.when(j_next < nj)
        def _():
            for c in fetch(e_next, j_next):
                c.start()


def _swiglu_rows(xs_ref, wg_scr, wu_scr, bg_ref, bu_ref, h_ref, m):
    xb = xs_ref[:m, :].astype(BF16)
    tn = h_ref.shape[1]
    for n0 in range(0, tn, MXU_N):
        cols = slice(n0, n0 + MXU_N)

        def proj(w_scr, b_ref):
            return jnp.dot(xb, w_scr[:, cols], preferred_element_type=F32) + b_ref[0, :, cols]

        gate = jnp.minimum(proj(wg_scr, bg_ref), SWIGLU_LIMIT)
        up = jnp.clip(proj(wu_scr, bu_ref), -SWIGLU_LIMIT, SWIGLU_LIMIT)
        glu = gate * jax.nn.sigmoid(SWIGLU_ALPHA * gate)
        h_ref[:m, cols] = ((up + 1.0) * glu).astype(h_ref.dtype)
    if m < h_ref.shape[0]:
        h_ref[m:, :] = jnp.zeros((h_ref.shape[0] - m, tn), h_ref.dtype)


def _expert_up_kernel(te_ref, tr_ref, ne_ref, nu_ref, xs_ref, w_hbm, bg_ref, bu_ref, h_ref,
                      stage, wg_scr, wu_scr, sems, *, nj):
    tn = h_ref.shape[1]
    F = nj * tn

    def fetch(e, j):
        c0 = pl.multiple_of(j * tn, tn)
        c1 = pl.multiple_of(F + j * tn, tn)
        return (pltpu.make_async_copy(w_hbm.at[e, :, pl.ds(c0, tn)], stage.at[0], sems.at[0]),
                pltpu.make_async_copy(w_hbm.at[e, :, pl.ds(c1, tn)], stage.at[1], sems.at[1]))

    def cast():
        def chunk(c, carry):
            r = pl.ds(pl.multiple_of(c * CAST_ROWS, CAST_ROWS), CAST_ROWS)
            wg_scr[r, :] = stage[0, r, :].astype(BF16)
            wu_scr[r, :] = stage[1, r, :].astype(BF16)
            return carry

        lax.fori_loop(0, wg_scr.shape[0] // CAST_ROWS, chunk, 0)

    _stream_expert_weights(te_ref, tr_ref, ne_ref, fetch, cast, nj)
    rows = tr_ref[pl.program_id(1)]
    args = (xs_ref, wg_scr, wu_scr, bg_ref, bu_ref, h_ref)

    @pl.when(rows > HALF_TILE)
    def _():
        _swiglu_rows(*args, EXPERT_TILE)

    @pl.when((rows > 0) & (rows <= HALF_TILE))
    def _():
        _swiglu_rows(*args, HALF_TILE)

    @pl.when(rows == 0)
    def _():
        h_ref[...] = jnp.zeros_like(h_ref)


def expert_up(tile_e, tile_rows, next_e, n_used, xs, w_gate_up, b_gate_up3, *, tm=EXPERT_TILE, tn=1024):
    P, D = xs.shape
    F = w_gate_up.shape[2] // 2
    nj = F // tn
    row = lambda i, nu: jnp.minimum(i, nu[0] - 1)
    return pl.pallas_call(
        functools.partial(_expert_up_kernel, nj=nj),
        grid_spec=pltpu.PrefetchScalarGridSpec(
            num_scalar_prefetch=4,
            grid=(nj, P // tm),
            in_specs=[
                pl.BlockSpec((tm, D), lambda j, i, te, tr, ne, nu: (row(i, nu), 0)),
                pl.BlockSpec(memory_space=pl.ANY),
                pl.BlockSpec((1, 1, tn), lambda j, i, te, tr, ne, nu: (te[i], 0, j)),
                pl.BlockSpec((1, 1, tn), lambda j, i, te, tr, ne, nu: (te[i], 0, nj + j)),
            ],
            out_specs=pl.BlockSpec((tm, tn), lambda j, i, te, tr, ne, nu: (i, j)),
            scratch_shapes=[
                pltpu.VMEM((2, D, tn), F32),
                pltpu.VMEM((D, tn), BF16),
                pltpu.VMEM((D, tn), BF16),
                pltpu.SemaphoreType.DMA((2,)),
            ],
        ),
        out_shape=jax.ShapeDtypeStruct((P, F), BF16),
        compiler_params=_params(("arbitrary", "arbitrary")),
        name="expert_up",
    )(tile_e, tile_rows, next_e, n_used, xs, w_gate_up, b_gate_up3, b_gate_up3)


def _down_rows(h_ref, wd_scr, bd_ref, y_ref, m):
    y_ref[:m, :] = jnp.dot(h_ref[:m, :], wd_scr[...], preferred_element_type=F32) + bd_ref[0]
    if m < y_ref.shape[0]:
        y_ref[m:, :] = jnp.zeros((y_ref.shape[0] - m, y_ref.shape[1]), y_ref.dtype)


def _expert_dn_kernel(te_ref, tr_ref, ne_ref, nu_ref, h_ref, w_hbm, bd_ref, y_ref, stage, wd_scr, sem, *, nj):
    tn = y_ref.shape[1]

    def fetch(e, j):
        c0 = pl.multiple_of(j * tn, tn)
        return (pltpu.make_async_copy(w_hbm.at[e, :, pl.ds(c0, tn)], stage, sem),)

    def cast():
        def chunk(c, carry):
            r = pl.ds(pl.multiple_of(c * CAST_ROWS, CAST_ROWS), CAST_ROWS)
            wd_scr[r, :] = stage[r, :].astype(BF16)
            return carry

        lax.fori_loop(0, wd_scr.shape[0] // CAST_ROWS, chunk, 0)

    _stream_expert_weights(te_ref, tr_ref, ne_ref, fetch, cast, nj)
    rows = tr_ref[pl.program_id(1)]

    @pl.when(rows > HALF_TILE)
    def _():
        _down_rows(h_ref, wd_scr, bd_ref, y_ref, EXPERT_TILE)

    @pl.when((rows > 0) & (rows <= HALF_TILE))
    def _():
        _down_rows(h_ref, wd_scr, bd_ref, y_ref, HALF_TILE)

    @pl.when(rows == 0)
    def _():
        y_ref[...] = jnp.zeros_like(y_ref)


def expert_dn(tile_e, tile_rows, next_e, n_used, h, w_down, b_down3, *, tm=EXPERT_TILE, tn=2048):
    P, F = h.shape
    D = w_down.shape[2]
    nj = D // tn
    row = lambda i, nu: jnp.minimum(i, nu[0] - 1)
    return pl.pallas_call(
        functools.partial(_expert_dn_kernel, nj=nj),
        grid_spec=pltpu.PrefetchScalarGridSpec(
            num_scalar_prefetch=4,
            grid=(nj, P // tm),
            in_specs=[
                pl.BlockSpec((tm, F), lambda j, i, te, tr, ne, nu: (row(i, nu), 0)),
                pl.BlockSpec(memory_space=pl.ANY),
                pl.BlockSpec((1, 1, tn), lambda j, i, te, tr, ne, nu: (te[i], 0, j)),
            ],
            out_specs=pl.BlockSpec((tm, tn), lambda j, i, te, tr, ne, nu: (i, j)),
            scratch_shapes=[
                pltpu.VMEM((F, tn), F32),
                pltpu.VMEM((F, tn), BF16),
                pltpu.SemaphoreType.DMA(()),
            ],
        ),
        out_shape=jax.ShapeDtypeStruct((P, D), F32),
        compiler_params=_params(("arbitrary", "arbitrary")),
        name="expert_dn",
    )(tile_e, tile_rows, next_e, n_used, h, w_down, b_down3)


def _combine_kernel(dcur_ref, dnxt_ref, x1_ref, gate_ref, gfin_ref, y_ref, o_ref, buf, sems, *, tm):
    i = pl.program_id(0)
    n = pl.num_programs(0)
    slot = i % 2

    def gather(dest_ref, s):
        def issue(t, carry):
            for k in range(TOP_K):
                d = dest_ref[t * TOP_K + k]
                pltpu.make_async_copy(y_ref.at[pl.ds(d, 1)], buf.at[s, k, pl.ds(t, 1)], sems.at[s]).start()
            return carry

        lax.fori_loop(0, tm, issue, 0, unroll=8)

    @pl.when(i == 0)
    def _():
        gather(dcur_ref, 0)

    @pl.when(i + 1 < n)
    def _():
        gather(dnxt_ref, 1 - slot)

    for k in range(TOP_K):
        pltpu.make_async_copy(y_ref.at[pl.ds(0, tm)], buf.at[slot, k], sems.at[slot]).wait()

    gates = gate_ref[...]
    moe = gates[:, 0:1] * buf[slot, 0]
    for k in range(1, TOP_K):
        moe = moe + gates[:, k:k + 1] * buf[slot, k]
    o_ref[...] = _rms(x1_ref[...] + moe, gfin_ref[...])


def combine(dest_flat, x1, gates, g_fin, y, *, tm=256):
    T, D = x1.shape
    n = T // tm
    blk = tm * TOP_K
    return pl.pallas_call(
        functools.partial(_combine_kernel, tm=tm),
        grid=(n,),
        in_specs=[
            pl.BlockSpec((blk,), lambda i: (i,), memory_space=pltpu.SMEM),
            pl.BlockSpec((blk,), lambda i: (jnp.minimum(i + 1, n - 1),), memory_space=pltpu.SMEM),
            pl.BlockSpec((tm, D), lambda i: (i, 0)),
            pl.BlockSpec((tm, LANES), lambda i: (i, 0)),
            pl.BlockSpec((1, D), lambda i: (0, 0)),
            pl.BlockSpec(memory_space=pl.ANY),
        ],
        out_specs=pl.BlockSpec((tm, D), lambda i: (i, 0)),
        out_shape=jax.ShapeDtypeStruct((T, D), F32),
        scratch_shapes=[pltpu.VMEM((2, TOP_K, tm, D), F32), pltpu.SemaphoreType.DMA((2,))],
        compiler_params=_params(("arbitrary",)),
        name="combine",
    )(dest_flat, dest_flat, x1, gates, g_fin, y)


def _tile_plan(counts, n_tiles, tile):
    tiles_e = (counts + tile - 1) // tile
    ends = jnp.cumsum(tiles_e)
    starts = ends - tiles_e
    pstart = starts * tile
    n_used = ends[-1]
    t = jnp.arange(n_tiles, dtype=I32)
    tc = jnp.minimum(t, n_used - 1)
    tile_e = jnp.sum((tc[:, None] >= ends[None, :]).astype(I32), axis=1)
    left = counts[tile_e] - (tc - starts[tile_e]) * tile
    tile_rows = jnp.where(t < n_used, jnp.clip(left, 0, tile), 0)
    nxt = ends[tile_e]
    next_e = jnp.where(nxt < n_used, tile_e[jnp.minimum(nxt, n_tiles - 1)], -1)
    spare = n_used + jnp.arange(counts.shape[0], dtype=I32)
    zrow = jnp.concatenate([jnp.where(tiles_e > 0, (ends - 1) * tile, -1),
                            jnp.where(spare < n_tiles, spare * tile, -1)])
    return (pstart.astype(I32), tile_e.astype(I32), tile_rows.astype(I32), next_e.astype(I32),
            n_used.reshape(1).astype(I32), zrow.astype(I32))


def kernel(x, norm_mix_g, w_in, gla_w_gate_up, gla_b_gate, gla_norm_g, lru_conv_w, lru_conv_b, lru_w_a,
           lru_b_a, lru_w_i, lru_b_i, lru_a_param, w_out, norm_ffn_g, w_router, b_router, w_gate_up,
           b_gate_up, w_down, b_down, norm_final_g):
    B, S, D = x.shape
    assert w_in.shape[0] == 1, "the final norm is fused into the single layer's combine stage"
    T = B * S
    x2 = x.reshape(T, D)
    kw = GLA_HEADS * GLA_DK
    gw = GLA_HEADS * GLA_DV
    lw = LRU_HEADS * LRU_BLOCK
    g0 = 2 * kw + 2 * gw

    w_in0 = w_in[0]
    w_main = jnp.concatenate([w_in0[:, :g0], w_in0[:, g0 + GLA_GATE_RANK:]], axis=1).astype(BF16)
    w_glow = jnp.pad(w_in0[:, g0:g0 + GLA_GATE_RANK], ((0, 0), (0, LANES - GLA_GATE_RANK))).astype(BF16)
    z, g_low = in_proj(x2, norm_mix_g[0].reshape(1, D), w_main, w_glow)

    wgu_pad = jnp.pad(gla_w_gate_up[0], ((0, LANES - GLA_GATE_RANK), (0, 0))).astype(BF16)
    gla_o = gla(z, g_low, wgu_pad, gla_b_gate[0].reshape(1, kw), gla_norm_g[0].reshape(1, gw),
                batch=B, seq=S)
    lru_o = lru(z, lru_conv_w[0], lru_conv_b[0].reshape(1, lw), lru_w_a[0].astype(BF16),
                lru_b_a[0].reshape(1, lw), lru_w_i[0].astype(BF16), lru_b_i[0].reshape(1, lw),
                lru_a_param[0].reshape(1, lw), batch=B, seq=S, lx_col=g0)

    wr_pad = jnp.pad(w_router[0], ((0, 0), (0, LANES - N_EXPERTS)))
    wr_hi = wr_pad.astype(BF16)
    wr_lo = (wr_pad - wr_hi.astype(F32)).astype(BF16)
    br_pad = jnp.pad(b_router[0], (0, LANES - N_EXPERTS)).reshape(1, LANES)
    x1, hf, idx_pad, gates, cnt = out_proj(gla_o, lru_o, x2, w_out[0].astype(BF16),
                                           norm_ffn_g[0].reshape(1, D), wr_hi, wr_lo, br_pad)

    n_tiles = (T * TOP_K) // EXPERT_TILE + N_EXPERTS
    pstart, tile_e, tile_rows, next_e, n_used, zrow = _tile_plan(
        cnt[0, :N_EXPERTS].astype(I32), n_tiles, EXPERT_TILE)
    ps_vec = jnp.pad(pstart.astype(F32), (0, LANES - N_EXPERTS)).reshape(1, LANES)
    dest_pad = rank(idx_pad, ps_vec)
    dest_flat = dest_pad[:, :TOP_K].reshape(-1)

    xs = scatter_rows(zrow, dest_flat, hf, n_tiles * EXPERT_TILE)
    E, _, F2 = w_gate_up.shape[1:]
    h = expert_up(tile_e, tile_rows, next_e, n_used, xs, w_gate_up[0], b_gate_up[0].reshape(E, 1, F2))
    y = expert_dn(tile_e, tile_rows, next_e, n_used, h, w_down[0], b_down[0].reshape(E, 1, D))
    out = combine(dest_flat, x1, gates, norm_final_g.reshape(1, D), y)
    return out.reshape(B, S, D)
```

```python
import functools

import jax
import jax.numpy as jnp
from jax import lax
from jax.experimental import pallas as pl
from jax.experimental.pallas import tpu as pltpu

F32 = jnp.float32
BF16 = jnp.bfloat16
I32 = jnp.int32

EPS = 1e-6
LANES = 128
SUBLANES = 8
MXU_N = 256
VMEM_LIMIT = 56 * 1024 * 1024

GLA_HEADS = 4
GLA_DK = 128
GLA_DV = 256
GLA_GATE_RANK = 16
GLA_GATE_NORM = 16.0
GLA_CHUNK = 64
GLA_HEADS_PER_STEP = 4
LRU_HEADS = 4
LRU_BLOCK = 256
LRU_CONV = 4
LRU_C = 8.0
N_EXPERTS = 32
TOP_K = 4
SWIGLU_LIMIT = 7.0
SWIGLU_ALPHA = 1.702

EXPERT_TILE = 512
ROW_STEP = 128
CAST_ROWS = 32


def _params(sem, vmem=VMEM_LIMIT):
    return pltpu.CompilerParams(dimension_semantics=sem, vmem_limit_bytes=vmem)


def _rms(x, g):
    ms = jnp.mean(x * x, axis=-1, keepdims=True)
    return x * lax.rsqrt(ms + EPS) * g


def _split3(x):
    a1 = x.astype(BF16)
    r1 = x - a1.astype(F32)
    a2 = r1.astype(BF16)
    a3 = (r1 - a2.astype(F32)).astype(BF16)
    return a1, a2, a3


def _in_proj_kernel(x_ref, g_ref, w_ref, wgl_ref, z_ref, gl_ref, h_scr):
    @pl.when(pl.program_id(1) == 0)
    def _():
        h = _rms(x_ref[...], g_ref[...]).astype(BF16)
        h_scr[...] = h
        gl_ref[...] = jnp.dot(h, wgl_ref[...], preferred_element_type=F32)

    z_ref[...] = jnp.dot(h_scr[...], w_ref[...], preferred_element_type=F32).astype(z_ref.dtype)


def in_proj(x2, g, w_main, w_glow, *, tm=1024, tn=1024):
    T, D = x2.shape
    N = w_main.shape[1]
    return pl.pallas_call(
        _in_proj_kernel,
        grid=(T // tm, N // tn),
        in_specs=[
            pl.BlockSpec((tm, D), lambda i, j: (i, 0)),
            pl.BlockSpec((1, D), lambda i, j: (0, 0)),
            pl.BlockSpec((D, tn), lambda i, j: (0, j)),
            pl.BlockSpec((D, LANES), lambda i, j: (0, 0)),
        ],
        out_specs=[
            pl.BlockSpec((tm, tn), lambda i, j: (i, j)),
            pl.BlockSpec((tm, LANES), lambda i, j: (i, 0)),
        ],
        out_shape=[jax.ShapeDtypeStruct((T, N), BF16), jax.ShapeDtypeStruct((T, LANES), F32)],
        scratch_shapes=[pltpu.VMEM((tm, D), BF16)],
        compiler_params=_params(("parallel", "arbitrary")),
        name="in_proj",
    )(x2, g, w_main, w_glow)


def _gla_kernel(q_ref, k_ref, v_ref, r_ref, gl_ref, wgu_ref, bg_ref, ng_ref, o_ref, st_scr, *, tc):
    @pl.when(pl.program_id(2) == 0)
    def _():
        st_scr[...] = jnp.zeros_like(st_scr)

    ri = lax.broadcasted_iota(I32, (tc, tc), 0)
    ci = lax.broadcasted_iota(I32, (tc, tc), 1)
    same = (ri // GLA_CHUNK) == (ci // GLA_CHUNK)
    cum_m = jnp.where(same & (ci <= ri), 1.0, 0.0).astype(BF16)
    tri = (lax.broadcasted_iota(I32, (GLA_CHUNK, GLA_CHUNK), 1)
           <= lax.broadcasted_iota(I32, (GLA_CHUNK, GLA_CHUNK), 0))
    scale = GLA_DK ** -0.5
    nt = (((1,), (1,)), ((), ()))
    tn = (((0,), (0,)), ((), ()))
    chunks = range(tc // GLA_CHUNK)
    rows = [slice(c * GLA_CHUNK, (c + 1) * GLA_CHUNK) for c in chunks]
    pre = jnp.dot(gl_ref[...].astype(BF16), wgu_ref[...], preferred_element_type=F32) + bg_ref[...]
    log_g = jax.nn.log_sigmoid(pre) * (1.0 / GLA_GATE_NORM)
    bcum_all = None
    for piece in _split3(log_g):
        part = jnp.dot(cum_m, piece, preferred_element_type=F32)
        bcum_all = part if bcum_all is None else bcum_all + part

    for hd in range(GLA_HEADS_PER_STEP):
        kc = slice(hd * GLA_DK, (hd + 1) * GLA_DK)
        vc = slice(hd * GLA_DV, (hd + 1) * GLA_DV)
        qe, ke, kd, dec = [], [], [], []
        for sl in rows:
            b = bcum_all[sl, kc]
            bl = b[GLA_CHUNK - 1:GLA_CHUNK]
            qf = q_ref[sl, kc].astype(F32) * scale
            kf = k_ref[sl, kc].astype(F32)
            qe.append((qf * jnp.exp(b)).astype(BF16))
            ke.append((kf * jnp.exp(-b)).astype(BF16))
            kd.append((kf * jnp.exp(bl - b)).astype(BF16))
            dec.append(jnp.exp(bl))
        scores = [jnp.where(tri, lax.dot_general(qe[c], ke[c], nt, preferred_element_type=F32), 0.0).astype(BF16)
                  for c in chunks]
        st = st_scr[hd]
        outs = []
        for c in chunks:
            vb = v_ref[rows[c], vc]
            o = jnp.dot(scores[c], vb, preferred_element_type=F32)
            o = o + lax.dot_general(qe[c], st.astype(BF16), nt, preferred_element_type=F32)
            st = st * dec[c] + lax.dot_general(vb, kd[c], tn, preferred_element_type=F32)
            outs.append(o)
        st_scr[hd] = st
        o = jnp.concatenate(outs, axis=0)
        on = _rms(o, ng_ref[:, vc])
        r = r_ref[:, vc].astype(F32)
        o_ref[:, vc] = (jax.nn.silu(r) * on).astype(o_ref.dtype)


def gla(z, g_low, wgu_pad, b_gate, norm_g, *, batch, seq, tc=256):
    T = z.shape[0]
    nc = seq // tc
    hp = GLA_HEADS_PER_STEP
    dk, dv = hp * GLA_DK, hp * GLA_DV
    kw = GLA_HEADS * GLA_DK
    k0 = kw // dk
    v0 = (2 * kw) // dv
    r0 = v0 + GLA_HEADS // hp
    row = lambda b, h, c: b * nc + c
    return pl.pallas_call(
        functools.partial(_gla_kernel, tc=tc),
        grid=(batch, GLA_HEADS // hp, nc),
        in_specs=[
            pl.BlockSpec((tc, dk), lambda b, h, c: (row(b, h, c), h)),
            pl.BlockSpec((tc, dk), lambda b, h, c: (row(b, h, c), k0 + h)),
            pl.BlockSpec((tc, dv), lambda b, h, c: (row(b, h, c), v0 + h)),
            pl.BlockSpec((tc, dv), lambda b, h, c: (row(b, h, c), r0 + h)),
            pl.BlockSpec((tc, LANES), lambda b, h, c: (row(b, h, c), 0)),
            pl.BlockSpec((LANES, dk), lambda b, h, c: (0, h)),
            pl.BlockSpec((1, dk), lambda b, h, c: (0, h)),
            pl.BlockSpec((1, dv), lambda b, h, c: (0, h)),
        ],
        out_specs=pl.BlockSpec((tc, dv), lambda b, h, c: (row(b, h, c), h)),
        out_shape=jax.ShapeDtypeStruct((T, GLA_HEADS * GLA_DV), BF16),
        scratch_shapes=[pltpu.VMEM((hp, GLA_DV, GLA_DK), F32)],
        compiler_params=_params(("parallel", "parallel", "arbitrary")),
        name="gla",
    )(z, z, z, z, g_low, wgu_pad, b_gate, norm_g)


def _lru_kernel(lx_ref, lg_ref, cw_ref, cb_ref, wa_ref, ba_ref, wi_ref, bi_ref, ap_ref, o_ref,
                tail_scr, h_scr, a_scr, u_scr, hs_scr, *, tc):
    @pl.when(pl.program_id(2) == 0)
    def _():
        tail_scr[...] = jnp.zeros_like(tail_scr)
        h_scr[...] = jnp.zeros_like(h_scr)

    x = lx_ref[...].astype(F32)
    tail_scr[SUBLANES:, :] = x
    xc = cb_ref[...]
    for i in range(LRU_CONV):
        s0 = SUBLANES - (LRU_CONV - 1) + i
        xc = xc + tail_scr[s0:s0 + tc, :] * cw_ref[i:i + 1, :]
    tail_scr[:SUBLANES, :] = x[tc - SUBLANES:, :]

    xb = xc.astype(BF16)
    ga = jax.nn.sigmoid(jnp.dot(xb, wa_ref[0], preferred_element_type=F32) + ba_ref[...])
    gi = jax.nn.sigmoid(jnp.dot(xb, wi_ref[0], preferred_element_type=F32) + bi_ref[...])
    log_a = -LRU_C * ga * jax.nn.softplus(-ap_ref[...])
    a = jnp.exp(log_a)
    u = xc * gi * jnp.sqrt(-jnp.tanh(log_a) * (a * a + 1.0))

    grouped = (tc // SUBLANES, SUBLANES, a.shape[1])
    a = a.reshape(grouped)
    u = u.reshape(grouped)
    sub = lax.broadcasted_iota(I32, grouped, 1)
    for s in (1, 2, 4):
        keep = sub >= s
        a_sh = jnp.where(keep, pltpu.roll(a, s, axis=1), 1.0)
        u_sh = jnp.where(keep, pltpu.roll(u, s, axis=1), 0.0)
        u = u + a * u_sh
        a = a * a_sh
    a_scr[...] = a.reshape(tc, grouped[2])
    u_scr[...] = u.reshape(tc, grouped[2])

    def body(g, hp):
        rows = pl.ds(pl.multiple_of(g * SUBLANES, SUBLANES), SUBLANES)
        hr = a_scr[rows, :] * hp + u_scr[rows, :]
        hs_scr[rows, :] = hr
        return jnp.broadcast_to(hr[SUBLANES - 1:, :], hr.shape)

    h_scr[...] = lax.fori_loop(0, tc // SUBLANES, body, h_scr[...], unroll=8)
    gate = jax.nn.gelu(lg_ref[...].astype(F32))
    o_ref[...] = (hs_scr[...] * gate).astype(o_ref.dtype)


def lru(z, conv_w, conv_b, w_a, b_a, w_i, b_i, a_param, *, batch, seq, lx_col, tc=512):
    T = z.shape[0]
    nc = seq // tc
    W = LRU_BLOCK
    lx0 = lx_col // W
    lg0 = lx0 + LRU_HEADS
    row = lambda b, h, c: b * nc + c
    vec = pl.BlockSpec((1, W), lambda b, h, c: (0, h))
    mat = pl.BlockSpec((1, W, W), lambda b, h, c: (h, 0, 0))
    return pl.pallas_call(
        functools.partial(_lru_kernel, tc=tc),
        grid=(batch, LRU_HEADS, nc),
        in_specs=[
            pl.BlockSpec((tc, W), lambda b, h, c: (row(b, h, c), lx0 + h)),
            pl.BlockSpec((tc, W), lambda b, h, c: (row(b, h, c), lg0 + h)),
            pl.BlockSpec((LRU_CONV, W), lambda b, h, c: (0, h)),
            vec, mat, vec, mat, vec, vec,
        ],
        out_specs=pl.BlockSpec((tc, W), lambda b, h, c: (row(b, h, c), h)),
        out_shape=jax.ShapeDtypeStruct((T, LRU_HEADS * W), BF16),
        scratch_shapes=[
            pltpu.VMEM((SUBLANES + tc, W), F32),
            pltpu.VMEM((SUBLANES, W), F32),
            pltpu.VMEM((tc, W), F32),
            pltpu.VMEM((tc, W), F32),
            pltpu.VMEM((tc, W), F32),
        ],
        compiler_params=_params(("parallel", "parallel", "arbitrary")),
        name="lru",
    )(z, z, conv_w, conv_b, w_a, b_a, w_i, b_i, a_param)


def _out_proj_kernel(go_ref, lo_ref, x_ref, wo_ref, gf_ref, wrh_ref, wrl_ref, br_ref,
                     x1_ref, hf_ref, idx_ref, gate_ref, cnt_ref):
    @pl.when(pl.program_id(0) == 0)
    def _():
        cnt_ref[...] = jnp.zeros_like(cnt_ref)

    half = go_ref.shape[1]
    mix = jnp.dot(go_ref[...], wo_ref[:half, :], preferred_element_type=F32)
    mix = mix + jnp.dot(lo_ref[...], wo_ref[half:, :], preferred_element_type=F32)
    x1 = x_ref[...] + mix
    x1_ref[...] = x1
    hf = _rms(x1, gf_ref[...])
    hf_ref[...] = hf
    hf_hi = hf.astype(BF16)
    hf_lo = (hf - hf_hi.astype(F32)).astype(BF16)

    logits = jnp.dot(hf_hi, wrh_ref[...], preferred_element_type=F32)
    logits = logits + jnp.dot(hf_lo, wrh_ref[...], preferred_element_type=F32)
    logits = logits + jnp.dot(hf_hi, wrl_ref[...], preferred_element_type=F32) + br_ref[...]
    lane = lax.broadcasted_iota(I32, logits.shape, 1)
    lanef = lane.astype(F32)
    neg = jnp.float32(-jnp.inf)
    work = jnp.where(lane < N_EXPERTS, logits, neg)
    vals, idxs = [], []
    hot = jnp.zeros(logits.shape, F32)
    for _ in range(TOP_K):
        m = jnp.max(work, axis=-1, keepdims=True)
        sel = jnp.min(jnp.where(work == m, lanef, float(LANES)), axis=-1, keepdims=True)
        picked = lanef == sel
        work = jnp.where(picked, neg, work)
        hot = jnp.where(picked, 1.0, hot)
        vals.append(m)
        idxs.append(sel)
    ex = [jnp.exp(v - vals[0]) for v in vals]
    den = ex[0] + ex[1] + ex[2] + ex[3]
    idx_out = jnp.zeros(logits.shape, F32)
    gate_out = jnp.zeros(logits.shape, F32)
    for k in range(TOP_K):
        idx_out = jnp.where(lane == k, idxs[k], idx_out)
        gate_out = jnp.where(lane == k, ex[k] / den, gate_out)
    idx_ref[...] = idx_out.astype(I32)
    gate_ref[...] = gate_out
    cnt_ref[...] += jnp.sum(hot, axis=0, keepdims=True)


def out_proj(gla_o, lru_o, x2, w_out, g_ffn, wr_hi, wr_lo, br_pad, *, tm=512):
    T, D = x2.shape
    half = gla_o.shape[1]
    row = lambda i: (i, 0)
    fixed = lambda i: (0, 0)
    return pl.pallas_call(
        _out_proj_kernel,
        grid=(T // tm,),
        in_specs=[
            pl.BlockSpec((tm, half), row),
            pl.BlockSpec((tm, half), row),
            pl.BlockSpec((tm, D), row),
            pl.BlockSpec((2 * half, D), fixed),
            pl.BlockSpec((1, D), fixed),
            pl.BlockSpec((D, LANES), fixed),
            pl.BlockSpec((D, LANES), fixed),
            pl.BlockSpec((1, LANES), fixed),
        ],
        out_specs=[
            pl.BlockSpec((tm, D), row),
            pl.BlockSpec((tm, D), row),
            pl.BlockSpec((tm, LANES), row),
            pl.BlockSpec((tm, LANES), row),
            pl.BlockSpec((SUBLANES, LANES), fixed),
        ],
        out_shape=[
            jax.ShapeDtypeStruct((T, D), F32),
            jax.ShapeDtypeStruct((T, D), F32),
            jax.ShapeDtypeStruct((T, LANES), I32),
            jax.ShapeDtypeStruct((T, LANES), F32),
            jax.ShapeDtypeStruct((SUBLANES, LANES), F32),
        ],
        compiler_params=_params(("arbitrary",)),
        name="out_proj",
    )(gla_o, lru_o, x2, w_out, g_ffn, wr_hi, wr_lo, br_pad)


def _rank_kernel(idx_ref, ps_ref, dest_ref, carry_scr):
    @pl.when(pl.program_id(0) == 0)
    def _():
        carry_scr[...] = jnp.zeros_like(carry_scr)

    idx = idx_ref[...]
    tm = idx.shape[0]
    lane = lax.broadcasted_iota(I32, idx.shape, 1)
    hot = [idx[:, k:k + 1] == lane for k in range(TOP_K)]
    oh = jnp.zeros(idx.shape, F32)
    for k in range(TOP_K):
        oh = oh + jnp.where(hot[k], 1.0, 0.0)
    lower = (lax.broadcasted_iota(I32, (tm, tm), 1) < lax.broadcasted_iota(I32, (tm, tm), 0))
    before = jnp.dot(jnp.where(lower, 1.0, 0.0).astype(BF16), oh.astype(BF16),
                     preferred_element_type=F32)
    base = before + (carry_scr[0:1, :] + ps_ref[...])
    out = jnp.zeros(idx.shape, F32)
    for k in range(TOP_K):
        dk = jnp.sum(jnp.where(hot[k], base, 0.0), axis=-1, keepdims=True)
        out = jnp.where(lane == k, dk, out)
    dest_ref[...] = out.astype(I32)
    carry_scr[...] += jnp.sum(oh, axis=0, keepdims=True)


def rank(idx_pad, ps_vec, *, tm=512):
    T = idx_pad.shape[0]
    return pl.pallas_call(
        _rank_kernel,
        grid=(T // tm,),
        in_specs=[
            pl.BlockSpec((tm, LANES), lambda i: (i, 0)),
            pl.BlockSpec((1, LANES), lambda i: (0, 0)),
        ],
        out_specs=pl.BlockSpec((tm, LANES), lambda i: (i, 0)),
        out_shape=jax.ShapeDtypeStruct((T, LANES), I32),
        scratch_shapes=[pltpu.VMEM((SUBLANES, LANES), F32)],
        compiler_params=_params(("arbitrary",)),
        name="rank",
    )(idx_pad, ps_vec)


def _scatter_kernel(zrow_ref, dest_ref, hp_ref, xs_ref, zero_scr, sem, zsem, *, tm):
    @pl.when(pl.program_id(0) == 0)
    def _():
        zero_scr[...] = jnp.zeros_like(zero_scr)

        def zero_copy(e):
            r0 = pl.multiple_of(zrow_ref[e], SUBLANES)
            return pltpu.make_async_copy(zero_scr, xs_ref.at[pl.ds(r0, EXPERT_TILE)], zsem)

        def zstart(e, carry):
            @pl.when(zrow_ref[e] >= 0)
            def _():
                zero_copy(e).start()
            return carry

        def zwait(e, carry):
            @pl.when(zrow_ref[e] >= 0)
            def _():
                zero_copy(e).wait()
            return carry

        lax.fori_loop(0, zrow_ref.shape[0], zstart, 0)
        lax.fori_loop(0, zrow_ref.shape[0], zwait, 0)

    def issue(t, carry):
        for k in range(TOP_K):
            d = dest_ref[t * TOP_K + k]
            pltpu.make_async_copy(hp_ref.at[pl.ds(t, 1)], xs_ref.at[pl.ds(d, 1)], sem).start()
        return carry

    lax.fori_loop(0, tm, issue, 0, unroll=8)
    for k in range(TOP_K):
        pltpu.make_async_copy(hp_ref, xs_ref.at[pl.ds(0, tm)], sem).wait()


def scatter_rows(zrow, dest_flat, hp, n_rows, *, tm=1024):
    T, W = hp.shape
    return pl.pallas_call(
        functools.partial(_scatter_kernel, tm=tm),
        grid_spec=pltpu.PrefetchScalarGridSpec(
            num_scalar_prefetch=1,
            grid=(T // tm,),
            in_specs=[
                pl.BlockSpec((tm * TOP_K,), lambda i, zr: (i,), memory_space=pltpu.SMEM),
                pl.BlockSpec((tm, W), lambda i, zr: (i, 0)),
            ],
            out_specs=pl.BlockSpec(memory_space=pl.ANY),
            scratch_shapes=[
                pltpu.VMEM((EXPERT_TILE, W), hp.dtype),
                pltpu.SemaphoreType.DMA(()),
                pltpu.SemaphoreType.DMA(()),
            ],
        ),
        out_shape=jax.ShapeDtypeStruct((n_rows, W), hp.dtype),
        compiler_params=_params(("arbitrary",)),
        name="scatter_rows",
    )(zrow, dest_flat, hp)


def _tile_changed(te_ref, i):
    return (i == 0) | (te_ref[i] != te_ref[jnp.maximum(i - 1, 0)])


def _stream_expert_weights(te_ref, tr_ref, ne_ref, fetch, cast, nj):
    j = pl.program_id(0)
    i = pl.program_id(1)

    @pl.when((tr_ref[i] > 0) & _tile_changed(te_ref, i))
    def _():
        @pl.when((j == 0) & (i == 0))
        def _():
            for c in fetch(te_ref[0], 0):
                c.start()

        for c in fetch(te_ref[i], j):
            c.wait()
        cast()
        wrap = ne_ref[i] < 0
        e_next = jnp.where(wrap, te_ref[0], ne_ref[i])
        j_next = jnp.where(wrap, j + 1, j)

        @pl.when(j_next < nj)
        def _():
            for c in fetch(e_next, j_next):
                c.start()


def _for_row_count(rows, out_ref, compute):
    for m in range(ROW_STEP, EXPERT_TILE + 1, ROW_STEP):
        @pl.when((rows > m - ROW_STEP) & (rows <= m))
        def _(m=m):
            compute(m)

    @pl.when(rows == 0)
    def _():
        out_ref[...] = jnp.zeros_like(out_ref)


def _swiglu_rows(xs_ref, wg_scr, wu_scr, bg_ref, bu_ref, h_ref, m):
    xb = xs_ref[:m, :].astype(BF16)
    tn = h_ref.shape[1]
    for n0 in range(0, tn, MXU_N):
        cols = slice(n0, n0 + MXU_N)

        def proj(w_scr, b_ref):
            return jnp.dot(xb, w_scr[:, cols], preferred_element_type=F32) + b_ref[0, :, cols]

        gate = jnp.minimum(proj(wg_scr, bg_ref), SWIGLU_LIMIT)
        up = jnp.clip(proj(wu_scr, bu_ref), -SWIGLU_LIMIT, SWIGLU_LIMIT)
        glu = gate * jax.nn.sigmoid(SWIGLU_ALPHA * gate)
        h_ref[:m, cols] = ((up + 1.0) * glu).astype(h_ref.dtype)
    if m < h_ref.shape[0]:
        h_ref[m:, :] = jnp.zeros((h_ref.shape[0] - m, tn), h_ref.dtype)


def _expert_up_kernel(te_ref, tr_ref, ne_ref, nu_ref, xs_ref, w_hbm, bg_ref, bu_ref, h_ref,
                      stage, wg_scr, wu_scr, sems, *, nj):
    tn = h_ref.shape[1]
    F = nj * tn

    def fetch(e, j):
        c0 = pl.multiple_of(j * tn, tn)
        c1 = pl.multiple_of(F + j * tn, tn)
        return (pltpu.make_async_copy(w_hbm.at[e, :, pl.ds(c0, tn)], stage.at[0], sems.at[0]),
                pltpu.make_async_copy(w_hbm.at[e, :, pl.ds(c1, tn)], stage.at[1], sems.at[1]))

    def cast():
        def chunk(c, carry):
            r = pl.ds(pl.multiple_of(c * CAST_ROWS, CAST_ROWS), CAST_ROWS)
            wg_scr[r, :] = stage[0, r, :].astype(BF16)
            wu_scr[r, :] = stage[1, r, :].astype(BF16)
            return carry

        lax.fori_loop(0, wg_scr.shape[0] // CAST_ROWS, chunk, 0)

    _stream_expert_weights(te_ref, tr_ref, ne_ref, fetch, cast, nj)
    rows = tr_ref[pl.program_id(1)]
    _for_row_count(rows, h_ref, functools.partial(_swiglu_rows, xs_ref, wg_scr, wu_scr, bg_ref, bu_ref, h_ref))


def expert_up(tile_e, tile_rows, next_e, n_used, xs, w_gate_up, b_gate_up3, *, tm=EXPERT_TILE, tn=1024):
    P, D = xs.shape
    F = w_gate_up.shape[2] // 2
    nj = F // tn
    row = lambda i, nu: jnp.minimum(i, nu[0] - 1)
    return pl.pallas_call(
        functools.partial(_expert_up_kernel, nj=nj),
        grid_spec=pltpu.PrefetchScalarGridSpec(
            num_scalar_prefetch=4,
            grid=(nj, P // tm),
            in_specs=[
                pl.BlockSpec((tm, D), lambda j, i, te, tr, ne, nu: (row(i, nu), 0)),
                pl.BlockSpec(memory_space=pl.ANY),
                pl.BlockSpec((1, 1, tn), lambda j, i, te, tr, ne, nu: (te[i], 0, j)),
                pl.BlockSpec((1, 1, tn), lambda j, i, te, tr, ne, nu: (te[i], 0, nj + j)),
            ],
            out_specs=pl.BlockSpec((tm, tn), lambda j, i, te, tr, ne, nu: (i, j)),
            scratch_shapes=[
                pltpu.VMEM((2, D, tn), F32),
                pltpu.VMEM((D, tn), BF16),
                pltpu.VMEM((D, tn), BF16),
                pltpu.SemaphoreType.DMA((2,)),
            ],
        ),
        out_shape=jax.ShapeDtypeStruct((P, F), BF16),
        compiler_params=_params(("arbitrary", "arbitrary")),
        name="expert_up",
    )(tile_e, tile_rows, next_e, n_used, xs, w_gate_up, b_gate_up3, b_gate_up3)


def _down_rows(h_ref, wd_scr, bd_ref, y_ref, m):
    y_ref[:m, :] = jnp.dot(h_ref[:m, :], wd_scr[...], preferred_element_type=F32) + bd_ref[0]
    if m < y_ref.shape[0]:
        y_ref[m:, :] = jnp.zeros((y_ref.shape[0] - m, y_ref.shape[1]), y_ref.dtype)


def _expert_dn_kernel(te_ref, tr_ref, ne_ref, nu_ref, h_ref, w_hbm, bd_ref, y_ref, stage, wd_scr, sem, *, nj):
    tn = y_ref.shape[1]

    def fetch(e, j):
        c0 = pl.multiple_of(j * tn, tn)
        return (pltpu.make_async_copy(w_hbm.at[e, :, pl.ds(c0, tn)], stage, sem),)

    def cast():
        def chunk(c, carry):
            r = pl.ds(pl.multiple_of(c * CAST_ROWS, CAST_ROWS), CAST_ROWS)
            wd_scr[r, :] = stage[r, :].astype(BF16)
            return carry

        lax.fori_loop(0, wd_scr.shape[0] // CAST_ROWS, chunk, 0)

    _stream_expert_weights(te_ref, tr_ref, ne_ref, fetch, cast, nj)
    rows = tr_ref[pl.program_id(1)]
    _for_row_count(rows, y_ref, functools.partial(_down_rows, h_ref, wd_scr, bd_ref, y_ref))


def expert_dn(tile_e, tile_rows, next_e, n_used, h, w_down, b_down3, *, tm=EXPERT_TILE, tn=2048):
    P, F = h.shape
    D = w_down.shape[2]
    nj = D // tn
    row = lambda i, nu: jnp.minimum(i, nu[0] - 1)
    return pl.pallas_call(
        functools.partial(_expert_dn_kernel, nj=nj),
        grid_spec=pltpu.PrefetchScalarGridSpec(
            num_scalar_prefetch=4,
            grid=(nj, P // tm),
            in_specs=[
                pl.BlockSpec((tm, F), lambda j, i, te, tr, ne, nu: (row(i, nu), 0)),
                pl.BlockSpec(memory_space=pl.ANY),
                pl.BlockSpec((1, 1, tn), lambda j, i, te, tr, ne, nu: (te[i], 0, j)),
            ],
            out_specs=pl.BlockSpec((tm, tn), lambda j, i, te, tr, ne, nu: (i, j)),
            scratch_shapes=[
                pltpu.VMEM((F, tn), F32),
                pltpu.VMEM((F, tn), BF16),
                pltpu.SemaphoreType.DMA(()),
            ],
        ),
        out_shape=jax.ShapeDtypeStruct((P, D), F32),
        compiler_params=_params(("arbitrary", "arbitrary")),
        name="expert_dn",
    )(tile_e, tile_rows, next_e, n_used, h, w_down, b_down3)


def _combine_kernel(dcur_ref, dnxt_ref, x1_ref, gate_ref, gfin_ref, y_ref, o_ref, buf, sems, *, tm):
    i = pl.program_id(0)
    n = pl.num_programs(0)
    slot = i % 2

    def gather(dest_ref, s):
        def issue(t, carry):
            for k in range(TOP_K):
                d = dest_ref[t * TOP_K + k]
                pltpu.make_async_copy(y_ref.at[pl.ds(d, 1)], buf.at[s, k, pl.ds(t, 1)], sems.at[s]).start()
            return carry

        lax.fori_loop(0, tm, issue, 0, unroll=8)

    @pl.when(i == 0)
    def _():
        gather(dcur_ref, 0)

    @pl.when(i + 1 < n)
    def _():
        gather(dnxt_ref, 1 - slot)

    for k in range(TOP_K):
        pltpu.make_async_copy(y_ref.at[pl.ds(0, tm)], buf.at[slot, k], sems.at[slot]).wait()

    gates = gate_ref[...]
    moe = gates[:, 0:1] * buf[slot, 0]
    for k in range(1, TOP_K):
        moe = moe + gates[:, k:k + 1] * buf[slot, k]
    o_ref[...] = _rms(x1_ref[...] + moe, gfin_ref[...])


def combine(dest_flat, x1, gates, g_fin, y, *, tm=256):
    T, D = x1.shape
    n = T // tm
    blk = tm * TOP_K
    return pl.pallas_call(
        functools.partial(_combine_kernel, tm=tm),
        grid=(n,),
        in_specs=[
            pl.BlockSpec((blk,), lambda i: (i,), memory_space=pltpu.SMEM),
            pl.BlockSpec((blk,), lambda i: (jnp.minimum(i + 1, n - 1),), memory_space=pltpu.SMEM),
            pl.BlockSpec((tm, D), lambda i: (i, 0)),
            pl.BlockSpec((tm, LANES), lambda i: (i, 0)),
            pl.BlockSpec((1, D), lambda i: (0, 0)),
            pl.BlockSpec(memory_space=pl.ANY),
        ],
        out_specs=pl.BlockSpec((tm, D), lambda i: (i, 0)),
        out_shape=jax.ShapeDtypeStruct((T, D), F32),
        scratch_shapes=[pltpu.VMEM((2, TOP_K, tm, D), F32), pltpu.SemaphoreType.DMA((2,))],
        compiler_params=_params(("arbitrary",)),
        name="combine",
    )(dest_flat, dest_flat, x1, gates, g_fin, y)


def _tile_plan(counts, n_tiles, tile):
    tiles_e = (counts + tile - 1) // tile
    ends = jnp.cumsum(tiles_e)
    starts = ends - tiles_e
    pstart = starts * tile
    n_used = ends[-1]
    t = jnp.arange(n_tiles, dtype=I32)
    tc = jnp.minimum(t, n_used - 1)
    expert_of = lambda tiles: jnp.sum((tiles[:, None] >= ends[None, :]).astype(I32), axis=1)
    tile_e = expert_of(tc)
    mine = tile_e[:, None] == jnp.arange(counts.shape[0], dtype=I32)[None, :]
    pick = lambda table: jnp.sum(jnp.where(mine, table[None, :], 0), axis=1)
    left = pick(counts) - (tc - pick(starts)) * tile
    tile_rows = jnp.where(t < n_used, jnp.clip(left, 0, tile), 0)
    nxt = pick(ends)
    next_e = jnp.where(nxt < n_used, expert_of(nxt), -1)
    spare = n_used + jnp.arange(counts.shape[0], dtype=I32)
    zrow = jnp.concatenate([jnp.where(tiles_e > 0, (ends - 1) * tile, -1),
                            jnp.where(spare < n_tiles, spare * tile, -1)])
    return (pstart.astype(I32), tile_e.astype(I32), tile_rows.astype(I32), next_e.astype(I32),
            n_used.reshape(1).astype(I32), zrow.astype(I32))


def kernel(x, norm_mix_g, w_in, gla_w_gate_up, gla_b_gate, gla_norm_g, lru_conv_w, lru_conv_b, lru_w_a,
           lru_b_a, lru_w_i, lru_b_i, lru_a_param, w_out, norm_ffn_g, w_router, b_router, w_gate_up,
           b_gate_up, w_down, b_down, norm_final_g):
    B, S, D = x.shape
    assert w_in.shape[0] == 1, "the final norm is fused into the single layer's combine stage"
    T = B * S
    x2 = x.reshape(T, D)
    kw = GLA_HEADS * GLA_DK
    gw = GLA_HEADS * GLA_DV
    lw = LRU_HEADS * LRU_BLOCK
    g0 = 2 * kw + 2 * gw

    w_in0 = w_in[0]
    w_main = jnp.concatenate([w_in0[:, :g0], w_in0[:, g0 + GLA_GATE_RANK:]], axis=1).astype(BF16)
    w_glow = jnp.pad(w_in0[:, g0:g0 + GLA_GATE_RANK], ((0, 0), (0, LANES - GLA_GATE_RANK))).astype(BF16)
    z, g_low = in_proj(x2, norm_mix_g[0].reshape(1, D), w_main, w_glow)

    wgu_pad = jnp.pad(gla_w_gate_up[0], ((0, LANES - GLA_GATE_RANK), (0, 0))).astype(BF16)
    gla_o = gla(z, g_low, wgu_pad, gla_b_gate[0].reshape(1, kw), gla_norm_g[0].reshape(1, gw),
                batch=B, seq=S)
    lru_o = lru(z, lru_conv_w[0], lru_conv_b[0].reshape(1, lw), lru_w_a[0].astype(BF16),
                lru_b_a[0].reshape(1, lw), lru_w_i[0].astype(BF16), lru_b_i[0].reshape(1, lw),
                lru_a_param[0].reshape(1, lw), batch=B, seq=S, lx_col=g0)

    wr_pad = jnp.pad(w_router[0], ((0, 0), (0, LANES - N_EXPERTS)))
    wr_hi = wr_pad.astype(BF16)
    wr_lo = (wr_pad - wr_hi.astype(F32)).astype(BF16)
    br_pad = jnp.pad(b_router[0], (0, LANES - N_EXPERTS)).reshape(1, LANES)
    x1, hf, idx_pad, gates, cnt = out_proj(gla_o, lru_o, x2, w_out[0].astype(BF16),
                                           norm_ffn_g[0].reshape(1, D), wr_hi, wr_lo, br_pad)

    n_tiles = (T * TOP_K) // EXPERT_TILE + N_EXPERTS
    pstart, tile_e, tile_rows, next_e, n_used, zrow = _tile_plan(
        cnt[0, :N_EXPERTS].astype(I32), n_tiles, EXPERT_TILE)
    ps_vec = jnp.pad(pstart.astype(F32), (0, LANES - N_EXPERTS)).reshape(1, LANES)
    dest_pad = rank(idx_pad, ps_vec)
    dest_flat = dest_pad[:, :TOP_K].reshape(-1)

    xs = scatter_rows(zrow, dest_flat, hf, n_tiles * EXPERT_TILE)
    E, _, F2 = w_gate_up.shape[1:]
    h = expert_up(tile_e, tile_rows, next_e, n_used, xs, w_gate_up[0], b_gate_up[0].reshape(E, 1, F2))
    y = expert_dn(tile_e, tile_rows, next_e, n_used, h, w_down[0], b_down[0].reshape(E, 1, D))
    out = combine(dest_flat, x1, gates, norm_final_g.reshape(1, D), y)
    return out.reshape(B, S, D)
```

```python
import functools

import jax
import jax.numpy as jnp
from jax import lax
from jax.experimental import pallas as pl
from jax.experimental.pallas import tpu as pltpu

F32 = jnp.float32
BF16 = jnp.bfloat16
I32 = jnp.int32

EPS = 1e-6
LANES = 128
SUBLANES = 8
MXU_N = 256
VMEM_LIMIT = 56 * 1024 * 1024

GLA_HEADS = 4
GLA_DK = 128
GLA_DV = 256
GLA_GATE_RANK = 16
GLA_GATE_NORM = 16.0
GLA_CHUNK = 64
GLA_HEADS_PER_STEP = 4
LRU_HEADS = 4
LRU_BLOCK = 256
LRU_CONV = 4
LRU_C = 8.0
N_EXPERTS = 32
TOP_K = 4
SWIGLU_LIMIT = 7.0
SWIGLU_ALPHA = 1.702

EXPERT_TILE = 512
ROW_STEP = 128
CAST_ROWS = 32


def _params(sem, vmem=VMEM_LIMIT):
    return pltpu.CompilerParams(dimension_semantics=sem, vmem_limit_bytes=vmem)


def _rms(x, g):
    ms = jnp.mean(x * x, axis=-1, keepdims=True)
    return x * lax.rsqrt(ms + EPS) * g


def _split3(x):
    a1 = x.astype(BF16)
    r1 = x - a1.astype(F32)
    a2 = r1.astype(BF16)
    a3 = (r1 - a2.astype(F32)).astype(BF16)
    return a1, a2, a3


def _in_proj_kernel(x_ref, g_ref, w_ref, wgl_ref, z_ref, gl_ref, h_scr):
    @pl.when(pl.program_id(1) == 0)
    def _():
        h = _rms(x_ref[...], g_ref[...]).astype(BF16)
        h_scr[...] = h
        gl_ref[...] = jnp.dot(h, wgl_ref[...], preferred_element_type=F32)

    z_ref[...] = jnp.dot(h_scr[...], w_ref[...], preferred_element_type=F32).astype(z_ref.dtype)


def in_proj(x2, g, w_main, w_glow, *, tm=1024, tn=1024):
    T, D = x2.shape
    N = w_main.shape[1]
    return pl.pallas_call(
        _in_proj_kernel,
        grid=(T // tm, N // tn),
        in_specs=[
            pl.BlockSpec((tm, D), lambda i, j: (i, 0)),
            pl.BlockSpec((1, D), lambda i, j: (0, 0)),
            pl.BlockSpec((D, tn), lambda i, j: (0, j)),
            pl.BlockSpec((D, LANES), lambda i, j: (0, 0)),
        ],
        out_specs=[
            pl.BlockSpec((tm, tn), lambda i, j: (i, j)),
            pl.BlockSpec((tm, LANES), lambda i, j: (i, 0)),
        ],
        out_shape=[jax.ShapeDtypeStruct((T, N), BF16), jax.ShapeDtypeStruct((T, LANES), F32)],
        scratch_shapes=[pltpu.VMEM((tm, D), BF16)],
        compiler_params=_params(("parallel", "arbitrary")),
        name="in_proj",
    )(x2, g, w_main, w_glow)


def _gla_kernel(q_ref, k_ref, v_ref, r_ref, gl_ref, wgu_ref, bg_ref, ng_ref, o_ref, st_scr, *, tc):
    @pl.when(pl.program_id(2) == 0)
    def _():
        st_scr[...] = jnp.zeros_like(st_scr)

    ri = lax.broadcasted_iota(I32, (tc, tc), 0)
    ci = lax.broadcasted_iota(I32, (tc, tc), 1)
    same = (ri // GLA_CHUNK) == (ci // GLA_CHUNK)
    cum_m = jnp.where(same & (ci <= ri), 1.0, 0.0).astype(BF16)
    tri = (lax.broadcasted_iota(I32, (GLA_CHUNK, GLA_CHUNK), 1)
           <= lax.broadcasted_iota(I32, (GLA_CHUNK, GLA_CHUNK), 0))
    scale = GLA_DK ** -0.5
    nt = (((1,), (1,)), ((), ()))
    tn = (((0,), (0,)), ((), ()))
    chunks = range(tc // GLA_CHUNK)
    rows = [slice(c * GLA_CHUNK, (c + 1) * GLA_CHUNK) for c in chunks]
    pre = jnp.dot(gl_ref[...].astype(BF16), wgu_ref[...], preferred_element_type=F32) + bg_ref[...]
    log_g = jax.nn.log_sigmoid(pre) * (1.0 / GLA_GATE_NORM)
    bcum_all = None
    for piece in _split3(log_g):
        part = jnp.dot(cum_m, piece, preferred_element_type=F32)
        bcum_all = part if bcum_all is None else bcum_all + part

    for hd in range(GLA_HEADS_PER_STEP):
        kc = slice(hd * GLA_DK, (hd + 1) * GLA_DK)
        vc = slice(hd * GLA_DV, (hd + 1) * GLA_DV)
        qe, ke, kd, dec = [], [], [], []
        for sl in rows:
            b = bcum_all[sl, kc]
            bl = b[GLA_CHUNK - 1:GLA_CHUNK]
            qf = q_ref[sl, kc].astype(F32) * scale
            kf = k_ref[sl, kc].astype(F32)
            qe.append((qf * jnp.exp(b)).astype(BF16))
            ke.append((kf * jnp.exp(-b)).astype(BF16))
            kd.append((kf * jnp.exp(bl - b)).astype(BF16))
            dec.append(jnp.exp(bl))
        scores = [jnp.where(tri, lax.dot_general(qe[c], ke[c], nt, preferred_element_type=F32), 0.0).astype(BF16)
                  for c in chunks]
        st = st_scr[hd]
        outs = []
        for c in chunks:
            vb = v_ref[rows[c], vc]
            o = jnp.dot(scores[c], vb, preferred_element_type=F32)
            o = o + lax.dot_general(qe[c], st.astype(BF16), nt, preferred_element_type=F32)
            st = st * dec[c] + lax.dot_general(vb, kd[c], tn, preferred_element_type=F32)
            outs.append(o)
        st_scr[hd] = st
        o = jnp.concatenate(outs, axis=0)
        on = _rms(o, ng_ref[:, vc])
        r = r_ref[:, vc].astype(F32)
        o_ref[:, vc] = (jax.nn.silu(r) * on).astype(o_ref.dtype)


def gla(z, g_low, wgu_pad, b_gate, norm_g, *, batch, seq, tc=256):
    T = z.shape[0]
    nc = seq // tc
    hp = GLA_HEADS_PER_STEP
    dk, dv = hp * GLA_DK, hp * GLA_DV
    kw = GLA_HEADS * GLA_DK
    k0 = kw // dk
    v0 = (2 * kw) // dv
    r0 = v0 + GLA_HEADS // hp
    row = lambda b, h, c: b * nc + c
    return pl.pallas_call(
        functools.partial(_gla_kernel, tc=tc),
        grid=(batch, GLA_HEADS // hp, nc),
        in_specs=[
            pl.BlockSpec((tc, dk), lambda b, h, c: (row(b, h, c), h)),
            pl.BlockSpec((tc, dk), lambda b, h, c: (row(b, h, c), k0 + h)),
            pl.BlockSpec((tc, dv), lambda b, h, c: (row(b, h, c), v0 + h)),
            pl.BlockSpec((tc, dv), lambda b, h, c: (row(b, h, c), r0 + h)),
            pl.BlockSpec((tc, LANES), lambda b, h, c: (row(b, h, c), 0)),
            pl.BlockSpec((LANES, dk), lambda b, h, c: (0, h)),
            pl.BlockSpec((1, dk), lambda b, h, c: (0, h)),
            pl.BlockSpec((1, dv), lambda b, h, c: (0, h)),
        ],
        out_specs=pl.BlockSpec((tc, dv), lambda b, h, c: (row(b, h, c), h)),
        out_shape=jax.ShapeDtypeStruct((T, GLA_HEADS * GLA_DV), BF16),
        scratch_shapes=[pltpu.VMEM((hp, GLA_DV, GLA_DK), F32)],
        compiler_params=_params(("parallel", "parallel", "arbitrary")),
        name="gla",
    )(z, z, z, z, g_low, wgu_pad, b_gate, norm_g)


def _lru_kernel(lx_ref, lg_ref, cw_ref, cb_ref, wa_ref, ba_ref, wi_ref, bi_ref, ap_ref, o_ref,
                tail_scr, h_scr, a_scr, u_scr, hs_scr, *, tc):
    @pl.when(pl.program_id(2) == 0)
    def _():
        tail_scr[...] = jnp.zeros_like(tail_scr)
        h_scr[...] = jnp.zeros_like(h_scr)

    x = lx_ref[...].astype(F32)
    tail_scr[SUBLANES:, :] = x
    xc = cb_ref[...]
    for i in range(LRU_CONV):
        s0 = SUBLANES - (LRU_CONV - 1) + i
        xc = xc + tail_scr[s0:s0 + tc, :] * cw_ref[i:i + 1, :]
    tail_scr[:SUBLANES, :] = x[tc - SUBLANES:, :]

    xb = xc.astype(BF16)
    ga = jax.nn.sigmoid(jnp.dot(xb, wa_ref[0], preferred_element_type=F32) + ba_ref[...])
    gi = jax.nn.sigmoid(jnp.dot(xb, wi_ref[0], preferred_element_type=F32) + bi_ref[...])
    log_a = -LRU_C * ga * jax.nn.softplus(-ap_ref[...])
    a = jnp.exp(log_a)
    u = xc * gi * jnp.sqrt(-jnp.tanh(log_a) * (a * a + 1.0))

    grouped = (tc // SUBLANES, SUBLANES, a.shape[1])
    a = a.reshape(grouped)
    u = u.reshape(grouped)
    sub = lax.broadcasted_iota(I32, grouped, 1)
    for s in (1, 2, 4):
        keep = sub >= s
        a_sh = jnp.where(keep, pltpu.roll(a, s, axis=1), 1.0)
        u_sh = jnp.where(keep, pltpu.roll(u, s, axis=1), 0.0)
        u = u + a * u_sh
        a = a * a_sh
    a_scr[...] = a.reshape(tc, grouped[2])
    u_scr[...] = u.reshape(tc, grouped[2])

    def body(g, hp):
        rows = pl.ds(pl.multiple_of(g * SUBLANES, SUBLANES), SUBLANES)
        hr = a_scr[rows, :] * hp + u_scr[rows, :]
        hs_scr[rows, :] = hr
        return jnp.broadcast_to(hr[SUBLANES - 1:, :], hr.shape)

    h_scr[...] = lax.fori_loop(0, tc // SUBLANES, body, h_scr[...], unroll=8)
    gate = jax.nn.gelu(lg_ref[...].astype(F32))
    o_ref[...] = (hs_scr[...] * gate).astype(o_ref.dtype)


def lru(z, conv_w, conv_b, w_a, b_a, w_i, b_i, a_param, *, batch, seq, lx_col, tc=512):
    T = z.shape[0]
    nc = seq // tc
    W = LRU_BLOCK
    lx0 = lx_col // W
    lg0 = lx0 + LRU_HEADS
    row = lambda b, h, c: b * nc + c
    vec = pl.BlockSpec((1, W), lambda b, h, c: (0, h))
    mat = pl.BlockSpec((1, W, W), lambda b, h, c: (h, 0, 0))
    return pl.pallas_call(
        functools.partial(_lru_kernel, tc=tc),
        grid=(batch, LRU_HEADS, nc),
        in_specs=[
            pl.BlockSpec((tc, W), lambda b, h, c: (row(b, h, c), lx0 + h)),
            pl.BlockSpec((tc, W), lambda b, h, c: (row(b, h, c), lg0 + h)),
            pl.BlockSpec((LRU_CONV, W), lambda b, h, c: (0, h)),
            vec, mat, vec, mat, vec, vec,
        ],
        out_specs=pl.BlockSpec((tc, W), lambda b, h, c: (row(b, h, c), h)),
        out_shape=jax.ShapeDtypeStruct((T, LRU_HEADS * W), BF16),
        scratch_shapes=[
            pltpu.VMEM((SUBLANES + tc, W), F32),
            pltpu.VMEM((SUBLANES, W), F32),
            pltpu.VMEM((tc, W), F32),
            pltpu.VMEM((tc, W), F32),
            pltpu.VMEM((tc, W), F32),
        ],
        compiler_params=_params(("parallel", "parallel", "arbitrary")),
        name="lru",
    )(z, z, conv_w, conv_b, w_a, b_a, w_i, b_i, a_param)


def _out_proj_kernel(go_ref, lo_ref, x_ref, wo_ref, gf_ref, wrh_ref, wrl_ref, br_ref,
                     x1_ref, hf_ref, idx_ref, gate_ref, cnt_ref):
    @pl.when(pl.program_id(0) == 0)
    def _():
        cnt_ref[...] = jnp.zeros_like(cnt_ref)

    half = go_ref.shape[1]
    mix = jnp.dot(go_ref[...], wo_ref[:half, :], preferred_element_type=F32)
    mix = mix + jnp.dot(lo_ref[...], wo_ref[half:, :], preferred_element_type=F32)
    x1 = x_ref[...] + mix
    x1_ref[...] = x1
    hf = _rms(x1, gf_ref[...])
    hf_ref[...] = hf
    hf_hi = hf.astype(BF16)
    hf_lo = (hf - hf_hi.astype(F32)).astype(BF16)

    logits = jnp.dot(hf_hi, wrh_ref[...], preferred_element_type=F32)
    logits = logits + jnp.dot(hf_lo, wrh_ref[...], preferred_element_type=F32)
    logits = logits + jnp.dot(hf_hi, wrl_ref[...], preferred_element_type=F32) + br_ref[...]
    lane = lax.broadcasted_iota(I32, logits.shape, 1)
    lanef = lane.astype(F32)
    neg = jnp.float32(-jnp.inf)
    work = jnp.where(lane < N_EXPERTS, logits, neg)
    vals, idxs = [], []
    hot = jnp.zeros(logits.shape, F32)
    for _ in range(TOP_K):
        m = jnp.max(work, axis=-1, keepdims=True)
        sel = jnp.min(jnp.where(work == m, lanef, float(LANES)), axis=-1, keepdims=True)
        picked = lanef == sel
        work = jnp.where(picked, neg, work)
        hot = jnp.where(picked, 1.0, hot)
        vals.append(m)
        idxs.append(sel)
    ex = [jnp.exp(v - vals[0]) for v in vals]
    den = ex[0] + ex[1] + ex[2] + ex[3]
    idx_out = jnp.zeros(logits.shape, F32)
    gate_out = jnp.zeros(logits.shape, F32)
    for k in range(TOP_K):
        idx_out = jnp.where(lane == k, idxs[k], idx_out)
        gate_out = jnp.where(lane == k, ex[k] / den, gate_out)
    idx_ref[...] = idx_out.astype(I32)
    gate_ref[...] = gate_out
    cnt_ref[...] += jnp.sum(hot, axis=0, keepdims=True)


def out_proj(gla_o, lru_o, x2, w_out, g_ffn, wr_hi, wr_lo, br_pad, *, tm=512):
    T, D = x2.shape
    half = gla_o.shape[1]
    row = lambda i: (i, 0)
    fixed = lambda i: (0, 0)
    return pl.pallas_call(
        _out_proj_kernel,
        grid=(T // tm,),
        in_specs=[
            pl.BlockSpec((tm, half), row),
            pl.BlockSpec((tm, half), row),
            pl.BlockSpec((tm, D), row),
            pl.BlockSpec((2 * half, D), fixed),
            pl.BlockSpec((1, D), fixed),
            pl.BlockSpec((D, LANES), fixed),
            pl.BlockSpec((D, LANES), fixed),
            pl.BlockSpec((1, LANES), fixed),
        ],
        out_specs=[
            pl.BlockSpec((tm, D), row),
            pl.BlockSpec((tm, D), row),
            pl.BlockSpec((tm, LANES), row),
            pl.BlockSpec((tm, LANES), row),
            pl.BlockSpec((SUBLANES, LANES), fixed),
        ],
        out_shape=[
            jax.ShapeDtypeStruct((T, D), F32),
            jax.ShapeDtypeStruct((T, D), F32),
            jax.ShapeDtypeStruct((T, LANES), I32),
            jax.ShapeDtypeStruct((T, LANES), F32),
            jax.ShapeDtypeStruct((SUBLANES, LANES), F32),
        ],
        compiler_params=_params(("arbitrary",)),
        name="out_proj",
    )(gla_o, lru_o, x2, w_out, g_ffn, wr_hi, wr_lo, br_pad)


def _rank_kernel(idx_ref, ps_ref, dest_ref, carry_scr):
    @pl.when(pl.program_id(0) == 0)
    def _():
        carry_scr[...] = jnp.zeros_like(carry_scr)

    idx = idx_ref[...]
    tm = idx.shape[0]
    lane = lax.broadcasted_iota(I32, idx.shape, 1)
    hot = [idx[:, k:k + 1] == lane for k in range(TOP_K)]
    oh = jnp.zeros(idx.shape, F32)
    for k in range(TOP_K):
        oh = oh + jnp.where(hot[k], 1.0, 0.0)
    lower = (lax.broadcasted_iota(I32, (tm, tm), 1) < lax.broadcasted_iota(I32, (tm, tm), 0))
    before = jnp.dot(jnp.where(lower, 1.0, 0.0).astype(BF16), oh.astype(BF16),
                     preferred_element_type=F32)
    base = before + (carry_scr[0:1, :] + ps_ref[...])
    out = jnp.zeros(idx.shape, F32)
    for k in range(TOP_K):
        dk = jnp.sum(jnp.where(hot[k], base, 0.0), axis=-1, keepdims=True)
        out = jnp.where(lane == k, dk, out)
    dest_ref[...] = out.astype(I32)
    carry_scr[...] += jnp.sum(oh, axis=0, keepdims=True)


def rank(idx_pad, ps_vec, *, tm=512):
    T = idx_pad.shape[0]
    return pl.pallas_call(
        _rank_kernel,
        grid=(T // tm,),
        in_specs=[
            pl.BlockSpec((tm, LANES), lambda i: (i, 0)),
            pl.BlockSpec((1, LANES), lambda i: (0, 0)),
        ],
        out_specs=pl.BlockSpec((tm, LANES), lambda i: (i, 0)),
        out_shape=jax.ShapeDtypeStruct((T, LANES), I32),
        scratch_shapes=[pltpu.VMEM((SUBLANES, LANES), F32)],
        compiler_params=_params(("arbitrary",)),
        name="rank",
    )(idx_pad, ps_vec)


def _scatter_kernel(zrow_ref, dest_ref, hp_ref, xs_ref, zero_scr, sem, zsem, *, tm):
    @pl.when(pl.program_id(0) == 0)
    def _():
        zero_scr[...] = jnp.zeros_like(zero_scr)

        def zero_copy(e):
            r0 = pl.multiple_of(zrow_ref[e], SUBLANES)
            return pltpu.make_async_copy(zero_scr, xs_ref.at[pl.ds(r0, EXPERT_TILE)], zsem)

        def zstart(e, carry):
            @pl.when(zrow_ref[e] >= 0)
            def _():
                zero_copy(e).start()
            return carry

        def zwait(e, carry):
            @pl.when(zrow_ref[e] >= 0)
            def _():
                zero_copy(e).wait()
            return carry

        lax.fori_loop(0, zrow_ref.shape[0], zstart, 0)
        lax.fori_loop(0, zrow_ref.shape[0], zwait, 0)

    def issue(t, carry):
        for k in range(TOP_K):
            d = dest_ref[t * TOP_K + k]
            pltpu.make_async_copy(hp_ref.at[pl.ds(t, 1)], xs_ref.at[pl.ds(d, 1)], sem).start()
        return carry

    lax.fori_loop(0, tm, issue, 0, unroll=8)
    for k in range(TOP_K):
        pltpu.make_async_copy(hp_ref, xs_ref.at[pl.ds(0, tm)], sem).wait()


def scatter_rows(zrow, dest_flat, hp, n_rows, *, tm=1024):
    T, W = hp.shape
    return pl.pallas_call(
        functools.partial(_scatter_kernel, tm=tm),
        grid_spec=pltpu.PrefetchScalarGridSpec(
            num_scalar_prefetch=1,
            grid=(T // tm,),
            in_specs=[
                pl.BlockSpec((tm * TOP_K,), lambda i, zr: (i,), memory_space=pltpu.SMEM),
                pl.BlockSpec((tm, W), lambda i, zr: (i, 0)),
            ],
            out_specs=pl.BlockSpec(memory_space=pl.ANY),
            scratch_shapes=[
                pltpu.VMEM((EXPERT_TILE, W), hp.dtype),
                pltpu.SemaphoreType.DMA(()),
                pltpu.SemaphoreType.DMA(()),
            ],
        ),
        out_shape=jax.ShapeDtypeStruct((n_rows, W), hp.dtype),
        compiler_params=_params(("arbitrary",)),
        name="scatter_rows",
    )(zrow, dest_flat, hp)


def _tile_changed(te_ref, i):
    return (i == 0) | (te_ref[i] != te_ref[jnp.maximum(i - 1, 0)])


def _grouped_step(te_ref, tr_ref, ne_ref, out_ref, fetch, cast_all, compute, nj):
    j = pl.program_id(0)
    i = pl.program_id(1)
    rows = tr_ref[i]
    first = (rows > 0) & _tile_changed(te_ref, i)
    full = rows > EXPERT_TILE - ROW_STEP
    later = jnp.logical_not(first)

    @pl.when(first)
    def _():
        @pl.when((j == 0) & (i == 0))
        def _():
            for c in fetch(te_ref[0], 0):
                c.start()

        for c in fetch(te_ref[i], j):
            c.wait()

    @pl.when(first & jnp.logical_not(full))
    def _():
        cast_all()

    for m in range(ROW_STEP, EXPERT_TILE, ROW_STEP):
        @pl.when((rows > m - ROW_STEP) & (rows <= m))
        def _(m=m):
            compute(m, False)

    @pl.when(full & later)
    def _():
        compute(EXPERT_TILE, False)

    @pl.when(full & first)
    def _():
        compute(EXPERT_TILE, True)

    @pl.when(rows == 0)
    def _():
        out_ref[...] = jnp.zeros_like(out_ref)

    @pl.when(first)
    def _():
        wrap = ne_ref[i] < 0
        e_next = jnp.where(wrap, te_ref[0], ne_ref[i])
        j_next = jnp.where(wrap, j + 1, j)

        @pl.when(j_next < nj)
        def _():
            for c in fetch(e_next, j_next):
                c.start()


def _cast_rows_loop(pairs):
    def chunk(c, carry):
        r = pl.ds(pl.multiple_of(c * CAST_ROWS, CAST_ROWS), CAST_ROWS)
        for src, dst in pairs:
            dst[r, :] = src[r, :].astype(BF16)
        return carry

    lax.fori_loop(0, pairs[0][1].shape[0] // CAST_ROWS, chunk, 0)


def _cast_cols(pairs, cols):
    for src, dst in pairs:
        for r0 in range(0, dst.shape[0], MXU_N):
            dst[r0:r0 + MXU_N, cols] = src[r0:r0 + MXU_N, cols].astype(BF16)


def _swiglu_rows(xs_ref, stage, wg_scr, wu_scr, bg_ref, bu_ref, h_ref, m, cast_inline):
    xb = xs_ref[:m, :].astype(BF16)
    tn = h_ref.shape[1]
    for n0 in range(0, tn, MXU_N):
        cols = slice(n0, n0 + MXU_N)
        if cast_inline:
            _cast_cols(((stage.at[0], wg_scr), (stage.at[1], wu_scr)), cols)

        def proj(w_scr, b_ref):
            return jnp.dot(xb, w_scr[:, cols], preferred_element_type=F32) + b_ref[0, :, cols]

        gate = jnp.minimum(proj(wg_scr, bg_ref), SWIGLU_LIMIT)
        up = jnp.clip(proj(wu_scr, bu_ref), -SWIGLU_LIMIT, SWIGLU_LIMIT)
        glu = gate * jax.nn.sigmoid(SWIGLU_ALPHA * gate)
        h_ref[:m, cols] = ((up + 1.0) * glu).astype(h_ref.dtype)
    if m < h_ref.shape[0]:
        h_ref[m:, :] = jnp.zeros((h_ref.shape[0] - m, tn), h_ref.dtype)


def _expert_up_kernel(te_ref, tr_ref, ne_ref, nu_ref, xs_ref, w_hbm, bg_ref, bu_ref, h_ref,
                      stage, wg_scr, wu_scr, sems, *, nj):
    tn = h_ref.shape[1]
    F = nj * tn

    def fetch(e, j):
        c0 = pl.multiple_of(j * tn, tn)
        c1 = pl.multiple_of(F + j * tn, tn)
        return (pltpu.make_async_copy(w_hbm.at[e, :, pl.ds(c0, tn)], stage.at[0], sems.at[0]),
                pltpu.make_async_copy(w_hbm.at[e, :, pl.ds(c1, tn)], stage.at[1], sems.at[1]))

    cast_all = functools.partial(_cast_rows_loop, ((stage.at[0], wg_scr), (stage.at[1], wu_scr)))
    compute = functools.partial(_swiglu_rows, xs_ref, stage, wg_scr, wu_scr, bg_ref, bu_ref, h_ref)
    _grouped_step(te_ref, tr_ref, ne_ref, h_ref, fetch, cast_all, compute, nj)


def expert_up(tile_e, tile_rows, next_e, n_used, xs, w_gate_up, b_gate_up3, *, tm=EXPERT_TILE, tn=1024):
    P, D = xs.shape
    F = w_gate_up.shape[2] // 2
    nj = F // tn
    row = lambda i, nu: jnp.minimum(i, nu[0] - 1)
    return pl.pallas_call(
        functools.partial(_expert_up_kernel, nj=nj),
        grid_spec=pltpu.PrefetchScalarGridSpec(
            num_scalar_prefetch=4,
            grid=(nj, P // tm),
            in_specs=[
                pl.BlockSpec((tm, D), lambda j, i, te, tr, ne, nu: (row(i, nu), 0)),
                pl.BlockSpec(memory_space=pl.ANY),
                pl.BlockSpec((1, 1, tn), lambda j, i, te, tr, ne, nu: (te[i], 0, j)),
                pl.BlockSpec((1, 1, tn), lambda j, i, te, tr, ne, nu: (te[i], 0, nj + j)),
            ],
            out_specs=pl.BlockSpec((tm, tn), lambda j, i, te, tr, ne, nu: (i, j)),
            scratch_shapes=[
                pltpu.VMEM((2, D, tn), F32),
                pltpu.VMEM((D, tn), BF16),
                pltpu.VMEM((D, tn), BF16),
                pltpu.SemaphoreType.DMA((2,)),
            ],
        ),
        out_shape=jax.ShapeDtypeStruct((P, F), BF16),
        compiler_params=_params(("arbitrary", "arbitrary")),
        name="expert_up",
    )(tile_e, tile_rows, next_e, n_used, xs, w_gate_up, b_gate_up3, b_gate_up3)


def _down_rows(h_ref, stage, wd_scr, bd_ref, y_ref, m, cast_inline):
    if cast_inline:
        hb = h_ref[:m, :]
        for n0 in range(0, y_ref.shape[1], MXU_N):
            cols = slice(n0, n0 + MXU_N)
            _cast_cols(((stage, wd_scr),), cols)
            y_ref[:m, cols] = jnp.dot(hb, wd_scr[:, cols], preferred_element_type=F32) + bd_ref[0, :, cols]
    else:
        y_ref[:m, :] = jnp.dot(h_ref[:m, :], wd_scr[...], preferred_element_type=F32) + bd_ref[0]
    if m < y_ref.shape[0]:
        y_ref[m:, :] = jnp.zeros((y_ref.shape[0] - m, y_ref.shape[1]), y_ref.dtype)


def _expert_dn_kernel(te_ref, tr_ref, ne_ref, nu_ref, h_ref, w_hbm, bd_ref, y_ref, stage, wd_scr, sem, *, nj):
    tn = y_ref.shape[1]

    def fetch(e, j):
        c0 = pl.multiple_of(j * tn, tn)
        return (pltpu.make_async_copy(w_hbm.at[e, :, pl.ds(c0, tn)], stage, sem),)

    cast_all = functools.partial(_cast_rows_loop, ((stage, wd_scr),))
    compute = functools.partial(_down_rows, h_ref, stage, wd_scr, bd_ref, y_ref)
    _grouped_step(te_ref, tr_ref, ne_ref, y_ref, fetch, cast_all, compute, nj)


def expert_dn(tile_e, tile_rows, next_e, n_used, h, w_down, b_down3, *, tm=EXPERT_TILE, tn=2048):
    P, F = h.shape
    D = w_down.shape[2]
    nj = D // tn
    row = lambda i, nu: jnp.minimum(i, nu[0] - 1)
    return pl.pallas_call(
        functools.partial(_expert_dn_kernel, nj=nj),
        grid_spec=pltpu.PrefetchScalarGridSpec(
            num_scalar_prefetch=4,
            grid=(nj, P // tm),
            in_specs=[
                pl.BlockSpec((tm, F), lambda j, i, te, tr, ne, nu: (row(i, nu), 0)),
                pl.BlockSpec(memory_space=pl.ANY),
                pl.BlockSpec((1, 1, tn), lambda j, i, te, tr, ne, nu: (te[i], 0, j)),
            ],
            out_specs=pl.BlockSpec((tm, tn), lambda j, i, te, tr, ne, nu: (i, j)),
            scratch_shapes=[
                pltpu.VMEM((F, tn), F32),
                pltpu.VMEM((F, tn), BF16),
                pltpu.SemaphoreType.DMA(()),
            ],
        ),
        out_shape=jax.ShapeDtypeStruct((P, D), F32),
        compiler_params=_params(("arbitrary", "arbitrary")),
        name="expert_dn",
    )(tile_e, tile_rows, next_e, n_used, h, w_down, b_down3)


def _combine_kernel(dcur_ref, dnxt_ref, x1_ref, gate_ref, gfin_ref, y_ref, o_ref, buf, sems, *, tm):
    i = pl.program_id(0)
    n = pl.num_programs(0)
    slot = i % 2

    def gather(dest_ref, s):
        def issue(t, carry):
            for k in range(TOP_K):
                d = dest_ref[t * TOP_K + k]
                pltpu.make_async_copy(y_ref.at[pl.ds(d, 1)], buf.at[s, k, pl.ds(t, 1)], sems.at[s]).start()
            return carry

        lax.fori_loop(0, tm, issue, 0, unroll=8)

    @pl.when(i == 0)
    def _():
        gather(dcur_ref, 0)

    @pl.when(i + 1 < n)
    def _():
        gather(dnxt_ref, 1 - slot)

    for k in range(TOP_K):
        pltpu.make_async_copy(y_ref.at[pl.ds(0, tm)], buf.at[slot, k], sems.at[slot]).wait()

    gates = gate_ref[...]
    moe = gates[:, 0:1] * buf[slot, 0]
    for k in range(1, TOP_K):
        moe = moe + gates[:, k:k + 1] * buf[slot, k]
    o_ref[...] = _rms(x1_ref[...] + moe, gfin_ref[...])


def combine(dest_flat, x1, gates, g_fin, y, *, tm=256):
    T, D = x1.shape
    n = T // tm
    blk = tm * TOP_K
    return pl.pallas_call(
        functools.partial(_combine_kernel, tm=tm),
        grid=(n,),
        in_specs=[
            pl.BlockSpec((blk,), lambda i: (i,), memory_space=pltpu.SMEM),
            pl.BlockSpec((blk,), lambda i: (jnp.minimum(i + 1, n - 1),), memory_space=pltpu.SMEM),
            pl.BlockSpec((tm, D), lambda i: (i, 0)),
            pl.BlockSpec((tm, LANES), lambda i: (i, 0)),
            pl.BlockSpec((1, D), lambda i: (0, 0)),
            pl.BlockSpec(memory_space=pl.ANY),
        ],
        out_specs=pl.BlockSpec((tm, D), lambda i: (i, 0)),
        out_shape=jax.ShapeDtypeStruct((T, D), F32),
        scratch_shapes=[pltpu.VMEM((2, TOP_K, tm, D), F32), pltpu.SemaphoreType.DMA((2,))],
        compiler_params=_params(("arbitrary",)),
        name="combine",
    )(dest_flat, dest_flat, x1, gates, g_fin, y)


def _tile_plan(counts, n_tiles, tile):
    tiles_e = (counts + tile - 1) // tile
    ends = jnp.cumsum(tiles_e)
    starts = ends - tiles_e
    pstart = starts * tile
    n_used = ends[-1]
    t = jnp.arange(n_tiles, dtype=I32)
    tc = jnp.minimum(t, n_used - 1)
    expert_of = lambda tiles: jnp.sum((tiles[:, None] >= ends[None, :]).astype(I32), axis=1)
    tile_e = expert_of(tc)
    mine = tile_e[:, None] == jnp.arange(counts.shape[0], dtype=I32)[None, :]
    pick = lambda table: jnp.sum(jnp.where(mine, table[None, :], 0), axis=1)
    left = pick(counts) - (tc - pick(starts)) * tile
    tile_rows = jnp.where(t < n_used, jnp.clip(left, 0, tile), 0)
    nxt = pick(ends)
    next_e = jnp.where(nxt < n_used, expert_of(nxt), -1)
    spare = n_used + jnp.arange(counts.shape[0], dtype=I32)
    zrow = jnp.concatenate([jnp.where(tiles_e > 0, (ends - 1) * tile, -1),
                            jnp.where(spare < n_tiles, spare * tile, -1)])
    return (pstart.astype(I32), tile_e.astype(I32), tile_rows.astype(I32), next_e.astype(I32),
            n_used.reshape(1).astype(I32), zrow.astype(I32))


def kernel(x, norm_mix_g, w_in, gla_w_gate_up, gla_b_gate, gla_norm_g, lru_conv_w, lru_conv_b, lru_w_a,
           lru_b_a, lru_w_i, lru_b_i, lru_a_param, w_out, norm_ffn_g, w_router, b_router, w_gate_up,
           b_gate_up, w_down, b_down, norm_final_g):
    B, S, D = x.shape
    assert w_in.shape[0] == 1, "the final norm is fused into the single layer's combine stage"
    T = B * S
    x2 = x.reshape(T, D)
    kw = GLA_HEADS * GLA_DK
    gw = GLA_HEADS * GLA_DV
    lw = LRU_HEADS * LRU_BLOCK
    g0 = 2 * kw + 2 * gw

    w_in0 = w_in[0]
    w_main = jnp.concatenate([w_in0[:, :g0], w_in0[:, g0 + GLA_GATE_RANK:]], axis=1).astype(BF16)
    w_glow = jnp.pad(w_in0[:, g0:g0 + GLA_GATE_RANK], ((0, 0), (0, LANES - GLA_GATE_RANK))).astype(BF16)
    z, g_low = in_proj(x2, norm_mix_g[0].reshape(1, D), w_main, w_glow)

    wgu_pad = jnp.pad(gla_w_gate_up[0], ((0, LANES - GLA_GATE_RANK), (0, 0))).astype(BF16)
    gla_o = gla(z, g_low, wgu_pad, gla_b_gate[0].reshape(1, kw), gla_norm_g[0].reshape(1, gw),
                batch=B, seq=S)
    lru_o = lru(z, lru_conv_w[0], lru_conv_b[0].reshape(1, lw), lru_w_a[0].astype(BF16),
                lru_b_a[0].reshape(1, lw), lru_w_i[0].astype(BF16), lru_b_i[0].reshape(1, lw),
                lru_a_param[0].reshape(1, lw), batch=B, seq=S, lx_col=g0)

    wr_pad = jnp.pad(w_router[0], ((0, 0), (0, LANES - N_EXPERTS)))
    wr_hi = wr_pad.astype(BF16)
    wr_lo = (wr_pad - wr_hi.astype(F32)).astype(BF16)
    br_pad = jnp.pad(b_router[0], (0, LANES - N_EXPERTS)).reshape(1, LANES)
    x1, hf, idx_pad, gates, cnt = out_proj(gla_o, lru_o, x2, w_out[0].astype(BF16),
                                           norm_ffn_g[0].reshape(1, D), wr_hi, wr_lo, br_pad)

    n_tiles = (T * TOP_K) // EXPERT_TILE + N_EXPERTS
    pstart, tile_e, tile_rows, next_e, n_used, zrow = _tile_plan(
        cnt[0, :N_EXPERTS].astype(I32), n_tiles, EXPERT_TILE)
    ps_vec = jnp.pad(pstart.astype(F32), (0, LANES - N_EXPERTS)).reshape(1, LANES)
    dest_pad = rank(idx_pad, ps_vec)
    dest_flat = dest_pad[:, :TOP_K].reshape(-1)

    xs = scatter_rows(zrow, dest_flat, hf, n_tiles * EXPERT_TILE)
    E, _, F2 = w_gate_up.shape[1:]
    h = expert_up(tile_e, tile_rows, next_e, n_used, xs, w_gate_up[0], b_gate_up[0].reshape(E, 1, F2))
    y = expert_dn(tile_e, tile_rows, next_e, n_used, h, w_down[0], b_down[0].reshape(E, 1, D))
    out = combine(dest_flat, x1, gates, norm_final_g.reshape(1, D), y)
    return out.reshape(B, S, D)
```

```python
import functools

import jax
import jax.numpy as jnp
from jax import lax
from jax.experimental import pallas as pl
from jax.experimental.pallas import tpu as pltpu

F32 = jnp.float32
BF16 = jnp.bfloat16
I32 = jnp.int32

EPS = 1e-6
LANES = 128
SUBLANES = 8
MXU_N = 256
VMEM_LIMIT = 56 * 1024 * 1024

GLA_HEADS = 4
GLA_DK = 128
GLA_DV = 256
GLA_GATE_RANK = 16
GLA_GATE_NORM = 16.0
GLA_CHUNK = 64
GLA_HEADS_PER_STEP = 4
LRU_HEADS = 4
LRU_BLOCK = 256
LRU_CONV = 4
LRU_C = 8.0
N_EXPERTS = 32
TOP_K = 4
SWIGLU_LIMIT = 7.0
SWIGLU_ALPHA = 1.702

EXPERT_TILE = 512
ROW_STEP = 128
CAST_ROWS = 32


def _params(sem, vmem=VMEM_LIMIT):
    return pltpu.CompilerParams(dimension_semantics=sem, vmem_limit_bytes=vmem)


def _rms(x, g):
    ms = jnp.mean(x * x, axis=-1, keepdims=True)
    return x * lax.rsqrt(ms + EPS) * g


def _split3(x):
    a1 = x.astype(BF16)
    r1 = x - a1.astype(F32)
    a2 = r1.astype(BF16)
    a3 = (r1 - a2.astype(F32)).astype(BF16)
    return a1, a2, a3


def _in_proj_kernel(x_ref, g_ref, w_ref, wgl_ref, z_ref, gl_ref, h_scr):
    @pl.when(pl.program_id(1) == 0)
    def _():
        h = _rms(x_ref[...], g_ref[...]).astype(BF16)
        h_scr[...] = h
        gl_ref[...] = jnp.dot(h, wgl_ref[...], preferred_element_type=F32)

    z_ref[...] = jnp.dot(h_scr[...], w_ref[...], preferred_element_type=F32).astype(z_ref.dtype)


def in_proj(x2, g, w_main, w_glow, *, tm=1024, tn=1024):
    T, D = x2.shape
    N = w_main.shape[1]
    return pl.pallas_call(
        _in_proj_kernel,
        grid=(T // tm, N // tn),
        in_specs=[
            pl.BlockSpec((tm, D), lambda i, j: (i, 0)),
            pl.BlockSpec((1, D), lambda i, j: (0, 0)),
            pl.BlockSpec((D, tn), lambda i, j: (0, j)),
            pl.BlockSpec((D, LANES), lambda i, j: (0, 0)),
        ],
        out_specs=[
            pl.BlockSpec((tm, tn), lambda i, j: (i, j)),
            pl.BlockSpec((tm, LANES), lambda i, j: (i, 0)),
        ],
        out_shape=[jax.ShapeDtypeStruct((T, N), BF16), jax.ShapeDtypeStruct((T, LANES), F32)],
        scratch_shapes=[pltpu.VMEM((tm, D), BF16)],
        compiler_params=_params(("parallel", "arbitrary")),
        name="in_proj",
    )(x2, g, w_main, w_glow)


def _gla_kernel(q_ref, k_ref, v_ref, r_ref, gl_ref, wgu_ref, bg_ref, ng_ref, o_ref, st_scr, *, tc):
    @pl.when(pl.program_id(2) == 0)
    def _():
        st_scr[...] = jnp.zeros_like(st_scr)

    ri = lax.broadcasted_iota(I32, (tc, tc), 0)
    ci = lax.broadcasted_iota(I32, (tc, tc), 1)
    same = (ri // GLA_CHUNK) == (ci // GLA_CHUNK)
    cum_m = jnp.where(same & (ci <= ri), 1.0, 0.0).astype(BF16)
    tri = (lax.broadcasted_iota(I32, (GLA_CHUNK, GLA_CHUNK), 1)
           <= lax.broadcasted_iota(I32, (GLA_CHUNK, GLA_CHUNK), 0))
    scale = GLA_DK ** -0.5
    nt = (((1,), (1,)), ((), ()))
    tn = (((0,), (0,)), ((), ()))
    chunks = range(tc // GLA_CHUNK)
    rows = [slice(c * GLA_CHUNK, (c + 1) * GLA_CHUNK) for c in chunks]
    pre = jnp.dot(gl_ref[...].astype(BF16), wgu_ref[...], preferred_element_type=F32) + bg_ref[...]
    log_g = jax.nn.log_sigmoid(pre) * (1.0 / GLA_GATE_NORM)
    bcum_all = None
    for piece in _split3(log_g):
        part = jnp.dot(cum_m, piece, preferred_element_type=F32)
        bcum_all = part if bcum_all is None else bcum_all + part

    for hd in range(GLA_HEADS_PER_STEP):
        kc = slice(hd * GLA_DK, (hd + 1) * GLA_DK)
        vc = slice(hd * GLA_DV, (hd + 1) * GLA_DV)
        qe, ke, kd, dec = [], [], [], []
        for sl in rows:
            b = bcum_all[sl, kc]
            bl = b[GLA_CHUNK - 1:GLA_CHUNK]
            qf = q_ref[sl, kc].astype(F32) * scale
            kf = k_ref[sl, kc].astype(F32)
            qe.append((qf * jnp.exp(b)).astype(BF16))
            ke.append((kf * jnp.exp(-b)).astype(BF16))
            kd.append((kf * jnp.exp(bl - b)).astype(BF16))
            dec.append(jnp.exp(bl))
        scores = [jnp.where(tri, lax.dot_general(qe[c], ke[c], nt, preferred_element_type=F32), 0.0).astype(BF16)
                  for c in chunks]
        st = st_scr[hd]
        outs = []
        for c in chunks:
            vb = v_ref[rows[c], vc]
            o = jnp.dot(scores[c], vb, preferred_element_type=F32)
            o = o + lax.dot_general(qe[c], st.astype(BF16), nt, preferred_element_type=F32)
            st = st * dec[c] + lax.dot_general(vb, kd[c], tn, preferred_element_type=F32)
            outs.append(o)
        st_scr[hd] = st
        o = jnp.concatenate(outs, axis=0)
        on = _rms(o, ng_ref[:, vc])
        r = r_ref[:, vc].astype(F32)
        o_ref[:, vc] = (jax.nn.silu(r) * on).astype(o_ref.dtype)


def gla(z, g_low, wgu_pad, b_gate, norm_g, *, batch, seq, tc=256):
    T = z.shape[0]
    nc = seq // tc
    hp = GLA_HEADS_PER_STEP
    dk, dv = hp * GLA_DK, hp * GLA_DV
    kw = GLA_HEADS * GLA_DK
    k0 = kw // dk
    v0 = (2 * kw) // dv
    r0 = v0 + GLA_HEADS // hp
    row = lambda b, h, c: b * nc + c
    return pl.pallas_call(
        functools.partial(_gla_kernel, tc=tc),
        grid=(batch, GLA_HEADS // hp, nc),
        in_specs=[
            pl.BlockSpec((tc, dk), lambda b, h, c: (row(b, h, c), h)),
            pl.BlockSpec((tc, dk), lambda b, h, c: (row(b, h, c), k0 + h)),
            pl.BlockSpec((tc, dv), lambda b, h, c: (row(b, h, c), v0 + h)),
            pl.BlockSpec((tc, dv), lambda b, h, c: (row(b, h, c), r0 + h)),
            pl.BlockSpec((tc, LANES), lambda b, h, c: (row(b, h, c), 0)),
            pl.BlockSpec((LANES, dk), lambda b, h, c: (0, h)),
            pl.BlockSpec((1, dk), lambda b, h, c: (0, h)),
            pl.BlockSpec((1, dv), lambda b, h, c: (0, h)),
        ],
        out_specs=pl.BlockSpec((tc, dv), lambda b, h, c: (row(b, h, c), h)),
        out_shape=jax.ShapeDtypeStruct((T, GLA_HEADS * GLA_DV), BF16),
        scratch_shapes=[pltpu.VMEM((hp, GLA_DV, GLA_DK), F32)],
        compiler_params=_params(("parallel", "parallel", "arbitrary")),
        name="gla",
    )(z, z, z, z, g_low, wgu_pad, b_gate, norm_g)


def _lru_kernel(lx_ref, lg_ref, cw_ref, cb_ref, wa_ref, ba_ref, wi_ref, bi_ref, ap_ref, o_ref,
                tail_scr, h_scr, a_scr, u_scr, hs_scr, *, tc):
    @pl.when(pl.program_id(2) == 0)
    def _():
        tail_scr[...] = jnp.zeros_like(tail_scr)
        h_scr[...] = jnp.zeros_like(h_scr)

    x = lx_ref[...].astype(F32)
    tail_scr[SUBLANES:, :] = x
    xc = cb_ref[...]
    for i in range(LRU_CONV):
        s0 = SUBLANES - (LRU_CONV - 1) + i
        xc = xc + tail_scr[s0:s0 + tc, :] * cw_ref[i:i + 1, :]
    tail_scr[:SUBLANES, :] = x[tc - SUBLANES:, :]

    xb = xc.astype(BF16)
    ga = jax.nn.sigmoid(jnp.dot(xb, wa_ref[0], preferred_element_type=F32) + ba_ref[...])
    gi = jax.nn.sigmoid(jnp.dot(xb, wi_ref[0], preferred_element_type=F32) + bi_ref[...])
    log_a = -LRU_C * ga * jax.nn.softplus(-ap_ref[...])
    a = jnp.exp(log_a)
    u = xc * gi * jnp.sqrt(-jnp.tanh(log_a) * (a * a + 1.0))

    grouped = (tc // SUBLANES, SUBLANES, a.shape[1])
    a = a.reshape(grouped)
    u = u.reshape(grouped)
    sub = lax.broadcasted_iota(I32, grouped, 1)
    for s in (1, 2, 4):
        keep = sub >= s
        a_sh = jnp.where(keep, pltpu.roll(a, s, axis=1), 1.0)
        u_sh = jnp.where(keep, pltpu.roll(u, s, axis=1), 0.0)
        u = u + a * u_sh
        a = a * a_sh
    a_scr[...] = a.reshape(tc, grouped[2])
    u_scr[...] = u.reshape(tc, grouped[2])

    def body(g, hp):
        rows = pl.ds(pl.multiple_of(g * SUBLANES, SUBLANES), SUBLANES)
        hr = a_scr[rows, :] * hp + u_scr[rows, :]
        hs_scr[rows, :] = hr
        return jnp.broadcast_to(hr[SUBLANES - 1:, :], hr.shape)

    h_scr[...] = lax.fori_loop(0, tc // SUBLANES, body, h_scr[...], unroll=8)
    gate = jax.nn.gelu(lg_ref[...].astype(F32))
    o_ref[...] = (hs_scr[...] * gate).astype(o_ref.dtype)


def lru(z, conv_w, conv_b, w_a, b_a, w_i, b_i, a_param, *, batch, seq, lx_col, tc=512):
    T = z.shape[0]
    nc = seq // tc
    W = LRU_BLOCK
    lx0 = lx_col // W
    lg0 = lx0 + LRU_HEADS
    row = lambda b, h, c: b * nc + c
    vec = pl.BlockSpec((1, W), lambda b, h, c: (0, h))
    mat = pl.BlockSpec((1, W, W), lambda b, h, c: (h, 0, 0))
    return pl.pallas_call(
        functools.partial(_lru_kernel, tc=tc),
        grid=(batch, LRU_HEADS, nc),
        in_specs=[
            pl.BlockSpec((tc, W), lambda b, h, c: (row(b, h, c), lx0 + h)),
            pl.BlockSpec((tc, W), lambda b, h, c: (row(b, h, c), lg0 + h)),
            pl.BlockSpec((LRU_CONV, W), lambda b, h, c: (0, h)),
            vec, mat, vec, mat, vec, vec,
        ],
        out_specs=pl.BlockSpec((tc, W), lambda b, h, c: (row(b, h, c), h)),
        out_shape=jax.ShapeDtypeStruct((T, LRU_HEADS * W), BF16),
        scratch_shapes=[
            pltpu.VMEM((SUBLANES + tc, W), F32),
            pltpu.VMEM((SUBLANES, W), F32),
            pltpu.VMEM((tc, W), F32),
            pltpu.VMEM((tc, W), F32),
            pltpu.VMEM((tc, W), F32),
        ],
        compiler_params=_params(("parallel", "parallel", "arbitrary")),
        name="lru",
    )(z, z, conv_w, conv_b, w_a, b_a, w_i, b_i, a_param)


def _out_proj_kernel(go_ref, lo_ref, x_ref, wo_ref, gf_ref, wrh_ref, wrl_ref, br_ref,
                     x1_ref, hf_ref, idx_ref, gate_ref, cnt_ref):
    @pl.when(pl.program_id(0) == 0)
    def _():
        cnt_ref[...] = jnp.zeros_like(cnt_ref)

    half = go_ref.shape[1]
    mix = jnp.dot(go_ref[...], wo_ref[:half, :], preferred_element_type=F32)
    mix = mix + jnp.dot(lo_ref[...], wo_ref[half:, :], preferred_element_type=F32)
    x1 = x_ref[...] + mix
    x1_ref[...] = x1
    hf = _rms(x1, gf_ref[...])
    hf_ref[...] = hf
    hf_hi = hf.astype(BF16)
    hf_lo = (hf - hf_hi.astype(F32)).astype(BF16)

    logits = jnp.dot(hf_hi, wrh_ref[...], preferred_element_type=F32)
    logits = logits + jnp.dot(hf_lo, wrh_ref[...], preferred_element_type=F32)
    logits = logits + jnp.dot(hf_hi, wrl_ref[...], preferred_element_type=F32) + br_ref[...]
    lane = lax.broadcasted_iota(I32, logits.shape, 1)
    lanef = lane.astype(F32)
    neg = jnp.float32(-jnp.inf)
    work = jnp.where(lane < N_EXPERTS, logits, neg)
    vals, idxs = [], []
    hot = jnp.zeros(logits.shape, F32)
    for _ in range(TOP_K):
        m = jnp.max(work, axis=-1, keepdims=True)
        sel = jnp.min(jnp.where(work == m, lanef, float(LANES)), axis=-1, keepdims=True)
        picked = lanef == sel
        work = jnp.where(picked, neg, work)
        hot = jnp.where(picked, 1.0, hot)
        vals.append(m)
        idxs.append(sel)
    ex = [jnp.exp(v - vals[0]) for v in vals]
    den = ex[0] + ex[1] + ex[2] + ex[3]
    idx_out = jnp.zeros(logits.shape, F32)
    gate_out = jnp.zeros(logits.shape, F32)
    for k in range(TOP_K):
        idx_out = jnp.where(lane == k, idxs[k], idx_out)
        gate_out = jnp.where(lane == k, ex[k] / den, gate_out)
    idx_ref[...] = idx_out.astype(I32)
    gate_ref[...] = gate_out
    cnt_ref[...] += jnp.sum(hot, axis=0, keepdims=True)


def out_proj(gla_o, lru_o, x2, w_out, g_ffn, wr_hi, wr_lo, br_pad, *, tm=512):
    T, D = x2.shape
    half = gla_o.shape[1]
    row = lambda i: (i, 0)
    fixed = lambda i: (0, 0)
    return pl.pallas_call(
        _out_proj_kernel,
        grid=(T // tm,),
        in_specs=[
            pl.BlockSpec((tm, half), row),
            pl.BlockSpec((tm, half), row),
            pl.BlockSpec((tm, D), row),
            pl.BlockSpec((2 * half, D), fixed),
            pl.BlockSpec((1, D), fixed),
            pl.BlockSpec((D, LANES), fixed),
            pl.BlockSpec((D, LANES), fixed),
            pl.BlockSpec((1, LANES), fixed),
        ],
        out_specs=[
            pl.BlockSpec((tm, D), row),
            pl.BlockSpec((tm, D), row),
            pl.BlockSpec((tm, LANES), row),
            pl.BlockSpec((tm, LANES), row),
            pl.BlockSpec((SUBLANES, LANES), fixed),
        ],
        out_shape=[
            jax.ShapeDtypeStruct((T, D), F32),
            jax.ShapeDtypeStruct((T, D), F32),
            jax.ShapeDtypeStruct((T, LANES), I32),
            jax.ShapeDtypeStruct((T, LANES), F32),
            jax.ShapeDtypeStruct((SUBLANES, LANES), F32),
        ],
        compiler_params=_params(("arbitrary",)),
        name="out_proj",
    )(gla_o, lru_o, x2, w_out, g_ffn, wr_hi, wr_lo, br_pad)


def _rank_kernel(idx_ref, ps_ref, dest_ref, carry_scr):
    @pl.when(pl.program_id(0) == 0)
    def _():
        carry_scr[...] = jnp.zeros_like(carry_scr)

    idx = idx_ref[...]
    tm = idx.shape[0]
    lane = lax.broadcasted_iota(I32, idx.shape, 1)
    hot = [idx[:, k:k + 1] == lane for k in range(TOP_K)]
    oh = jnp.zeros(idx.shape, F32)
    for k in range(TOP_K):
        oh = oh + jnp.where(hot[k], 1.0, 0.0)
    lower = (lax.broadcasted_iota(I32, (tm, tm), 1) < lax.broadcasted_iota(I32, (tm, tm), 0))
    before = jnp.dot(jnp.where(lower, 1.0, 0.0).astype(BF16), oh.astype(BF16),
                     preferred_element_type=F32)
    base = before + (carry_scr[0:1, :] + ps_ref[...])
    out = jnp.zeros(idx.shape, F32)
    for k in range(TOP_K):
        dk = jnp.sum(jnp.where(hot[k], base, 0.0), axis=-1, keepdims=True)
        out = jnp.where(lane == k, dk, out)
    dest_ref[...] = out.astype(I32)
    carry_scr[...] += jnp.sum(oh, axis=0, keepdims=True)


def rank(idx_pad, ps_vec, *, tm=512):
    T = idx_pad.shape[0]
    return pl.pallas_call(
        _rank_kernel,
        grid=(T // tm,),
        in_specs=[
            pl.BlockSpec((tm, LANES), lambda i: (i, 0)),
            pl.BlockSpec((1, LANES), lambda i: (0, 0)),
        ],
        out_specs=pl.BlockSpec((tm, LANES), lambda i: (i, 0)),
        out_shape=jax.ShapeDtypeStruct((T, LANES), I32),
        scratch_shapes=[pltpu.VMEM((SUBLANES, LANES), F32)],
        compiler_params=_params(("arbitrary",)),
        name="rank",
    )(idx_pad, ps_vec)


def _scatter_kernel(zrow_ref, dest_ref, hp_ref, xs_ref, zero_scr, sem, zsem, *, tm):
    @pl.when(pl.program_id(0) == 0)
    def _():
        zero_scr[...] = jnp.zeros_like(zero_scr)

        def zero_copy(e):
            r0 = pl.multiple_of(zrow_ref[e], SUBLANES)
            return pltpu.make_async_copy(zero_scr, xs_ref.at[pl.ds(r0, EXPERT_TILE)], zsem)

        def zstart(e, carry):
            @pl.when(zrow_ref[e] >= 0)
            def _():
                zero_copy(e).start()
            return carry

        def zwait(e, carry):
            @pl.when(zrow_ref[e] >= 0)
            def _():
                zero_copy(e).wait()
            return carry

        lax.fori_loop(0, zrow_ref.shape[0], zstart, 0)
        lax.fori_loop(0, zrow_ref.shape[0], zwait, 0)

    def issue(t, carry):
        for k in range(TOP_K):
            d = dest_ref[t * TOP_K + k]
            pltpu.make_async_copy(hp_ref.at[pl.ds(t, 1)], xs_ref.at[pl.ds(d, 1)], sem).start()
        return carry

    for t in range(tm):
        issue(t, 0)
    for k in range(TOP_K):
        pltpu.make_async_copy(hp_ref, xs_ref.at[pl.ds(0, tm)], sem).wait()


def scatter_rows(zrow, dest_flat, hp, n_rows, *, tm=512):
    T, W = hp.shape
    return pl.pallas_call(
        functools.partial(_scatter_kernel, tm=tm),
        grid_spec=pltpu.PrefetchScalarGridSpec(
            num_scalar_prefetch=1,
            grid=(T // tm,),
            in_specs=[
                pl.BlockSpec((tm * TOP_K,), lambda i, zr: (i,), memory_space=pltpu.SMEM),
                pl.BlockSpec((tm, W), lambda i, zr: (i, 0)),
            ],
            out_specs=pl.BlockSpec(memory_space=pl.ANY),
            scratch_shapes=[
                pltpu.VMEM((EXPERT_TILE, W), hp.dtype),
                pltpu.SemaphoreType.DMA(()),
                pltpu.SemaphoreType.DMA(()),
            ],
        ),
        out_shape=jax.ShapeDtypeStruct((n_rows, W), hp.dtype),
        compiler_params=_params(("arbitrary",)),
        name="scatter_rows",
    )(zrow, dest_flat, hp)


def _tile_changed(te_ref, i):
    return (i == 0) | (te_ref[i] != te_ref[jnp.maximum(i - 1, 0)])


def _stream_expert_weights(te_ref, tr_ref, ne_ref, fetch, cast, nj):
    j = pl.program_id(0)
    i = pl.program_id(1)

    @pl.when((tr_ref[i] > 0) & _tile_changed(te_ref, i))
    def _():
        @pl.when((j == 0) & (i == 0))
        def _():
            for c in fetch(te_ref[0], 0):
                c.start()

        for c in fetch(te_ref[i], j):
            c.wait()
        cast()
        wrap = ne_ref[i] < 0
        e_next = jnp.where(wrap, te_ref[0], ne_ref[i])
        j_next = jnp.where(wrap, j + 1, j)

        @pl.when(j_next < nj)
        def _():
            for c in fetch(e_next, j_next):
                c.start()


def _for_row_count(rows, out_ref, compute):
    for m in range(ROW_STEP, EXPERT_TILE + 1, ROW_STEP):
        @pl.when((rows > m - ROW_STEP) & (rows <= m))
        def _(m=m):
            compute(m)

    @pl.when(rows == 0)
    def _():
        out_ref[...] = jnp.zeros_like(out_ref)


def _swiglu_rows(xs_ref, wg_scr, wu_scr, bg_ref, bu_ref, h_ref, m):
    xb = xs_ref[:m, :].astype(BF16)
    tn = h_ref.shape[1]
    for n0 in range(0, tn, MXU_N):
        cols = slice(n0, n0 + MXU_N)

        def proj(w_scr, b_ref):
            return jnp.dot(xb, w_scr[:, cols], preferred_element_type=F32) + b_ref[0, :, cols]

        gate = jnp.minimum(proj(wg_scr, bg_ref), SWIGLU_LIMIT)
        up = jnp.clip(proj(wu_scr, bu_ref), -SWIGLU_LIMIT, SWIGLU_LIMIT)
        glu = gate * jax.nn.sigmoid(SWIGLU_ALPHA * gate)
        h_ref[:m, cols] = ((up + 1.0) * glu).astype(h_ref.dtype)
    if m < h_ref.shape[0]:
        h_ref[m:, :] = jnp.zeros((h_ref.shape[0] - m, tn), h_ref.dtype)


def _expert_up_kernel(te_ref, tr_ref, ne_ref, nu_ref, xs_ref, w_hbm, bg_ref, bu_ref, h_ref,
                      stage, wg_scr, wu_scr, sems, *, nj):
    tn = h_ref.shape[1]
    F = nj * tn

    def fetch(e, j):
        c0 = pl.multiple_of(j * tn, tn)
        c1 = pl.multiple_of(F + j * tn, tn)
        return (pltpu.make_async_copy(w_hbm.at[e, :, pl.ds(c0, tn)], stage.at[0], sems.at[0]),
                pltpu.make_async_copy(w_hbm.at[e, :, pl.ds(c1, tn)], stage.at[1], sems.at[1]))

    def cast():
        def chunk(c, carry):
            r = pl.ds(pl.multiple_of(c * CAST_ROWS, CAST_ROWS), CAST_ROWS)
            wg_scr[r, :] = stage[0, r, :].astype(BF16)
            wu_scr[r, :] = stage[1, r, :].astype(BF16)
            return carry

        lax.fori_loop(0, wg_scr.shape[0] // CAST_ROWS, chunk, 0)

    _stream_expert_weights(te_ref, tr_ref, ne_ref, fetch, cast, nj)
    rows = tr_ref[pl.program_id(1)]
    _for_row_count(rows, h_ref, functools.partial(_swiglu_rows, xs_ref, wg_scr, wu_scr, bg_ref, bu_ref, h_ref))


def expert_up(tile_e, tile_rows, next_e, n_used, xs, w_gate_up, b_gate_up3, *, tm=EXPERT_TILE, tn=1024):
    P, D = xs.shape
    F = w_gate_up.shape[2] // 2
    nj = F // tn
    row = lambda i, nu: jnp.minimum(i, nu[0] - 1)
    return pl.pallas_call(
        functools.partial(_expert_up_kernel, nj=nj),
        grid_spec=pltpu.PrefetchScalarGridSpec(
            num_scalar_prefetch=4,
            grid=(nj, P // tm),
            in_specs=[
                pl.BlockSpec((tm, D), lambda j, i, te, tr, ne, nu: (row(i, nu), 0)),
                pl.BlockSpec(memory_space=pl.ANY),
                pl.BlockSpec((1, 1, tn), lambda j, i, te, tr, ne, nu: (te[i], 0, j)),
                pl.BlockSpec((1, 1, tn), lambda j, i, te, tr, ne, nu: (te[i], 0, nj + j)),
            ],
            out_specs=pl.BlockSpec((tm, tn), lambda j, i, te, tr, ne, nu: (i, j)),
            scratch_shapes=[
                pltpu.VMEM((2, D, tn), F32),
                pltpu.VMEM((D, tn), BF16),
                pltpu.VMEM((D, tn), BF16),
                pltpu.SemaphoreType.DMA((2,)),
            ],
        ),
        out_shape=jax.ShapeDtypeStruct((P, F), BF16),
        compiler_params=_params(("arbitrary", "arbitrary")),
        name="expert_up",
    )(tile_e, tile_rows, next_e, n_used, xs, w_gate_up, b_gate_up3, b_gate_up3)


def _down_rows(h_ref, wd_scr, bd_ref, y_ref, m):
    y_ref[:m, :] = jnp.dot(h_ref[:m, :], wd_scr[...], preferred_element_type=F32) + bd_ref[0]
    if m < y_ref.shape[0]:
        y_ref[m:, :] = jnp.zeros((y_ref.shape[0] - m, y_ref.shape[1]), y_ref.dtype)


def _expert_dn_kernel(te_ref, tr_ref, ne_ref, nu_ref, h_ref, w_hbm, bd_ref, y_ref, stage, wd_scr, sem, *, nj):
    tn = y_ref.shape[1]

    def fetch(e, j):
        c0 = pl.multiple_of(j * tn, tn)
        return (pltpu.make_async_copy(w_hbm.at[e, :, pl.ds(c0, tn)], stage, sem),)

    def cast():
        def chunk(c, carry):
            r = pl.ds(pl.multiple_of(c * CAST_ROWS, CAST_ROWS), CAST_ROWS)
            wd_scr[r, :] = stage[r, :].astype(BF16)
            return carry

        lax.fori_loop(0, wd_scr.shape[0] // CAST_ROWS, chunk, 0)

    _stream_expert_weights(te_ref, tr_ref, ne_ref, fetch, cast, nj)
    rows = tr_ref[pl.program_id(1)]
    _for_row_count(rows, y_ref, functools.partial(_down_rows, h_ref, wd_scr, bd_ref, y_ref))


def expert_dn(tile_e, tile_rows, next_e, n_used, h, w_down, b_down3, *, tm=EXPERT_TILE, tn=2048):
    P, F = h.shape
    D = w_down.shape[2]
    nj = D // tn
    row = lambda i, nu: jnp.minimum(i, nu[0] - 1)
    return pl.pallas_call(
        functools.partial(_expert_dn_kernel, nj=nj),
        grid_spec=pltpu.PrefetchScalarGridSpec(
            num_scalar_prefetch=4,
            grid=(nj, P // tm),
            in_specs=[
                pl.BlockSpec((tm, F), lambda j, i, te, tr, ne, nu: (row(i, nu), 0)),
                pl.BlockSpec(memory_space=pl.ANY),
                pl.BlockSpec((1, 1, tn), lambda j, i, te, tr, ne, nu: (te[i], 0, j)),
            ],
            out_specs=pl.BlockSpec((tm, tn), lambda j, i, te, tr, ne, nu: (i, j)),
            scratch_shapes=[
                pltpu.VMEM((F, tn), F32),
                pltpu.VMEM((F, tn), BF16),
                pltpu.SemaphoreType.DMA(()),
            ],
        ),
        out_shape=jax.ShapeDtypeStruct((P, D), F32),
        compiler_params=_params(("arbitrary", "arbitrary")),
        name="expert_dn",
    )(tile_e, tile_rows, next_e, n_used, h, w_down, b_down3)


def _combine_kernel(dcur_ref, dnxt_ref, x1_ref, gate_ref, gfin_ref, y_ref, o_ref, buf, sems, *, tm):
    i = pl.program_id(0)
    n = pl.num_programs(0)
    slot = i % 2

    def gather(dest_ref, s, unrolled):
        def issue(t, carry):
            for k in range(TOP_K):
                d = dest_ref[t * TOP_K + k]
                pltpu.make_async_copy(y_ref.at[pl.ds(d, 1)], buf.at[s, k, pl.ds(t, 1)], sems.at[s]).start()
            return carry

        if unrolled:
            for t in range(tm):
                issue(t, 0)
        else:
            lax.fori_loop(0, tm, issue, 0, unroll=8)

    @pl.when(i == 0)
    def _():
        gather(dcur_ref, 0, False)

    @pl.when(i + 1 < n)
    def _():
        gather(dnxt_ref, 1 - slot, True)

    for k in range(TOP_K):
        pltpu.make_async_copy(y_ref.at[pl.ds(0, tm)], buf.at[slot, k], sems.at[slot]).wait()

    gates = gate_ref[...]
    moe = gates[:, 0:1] * buf[slot, 0]
    for k in range(1, TOP_K):
        moe = moe + gates[:, k:k + 1] * buf[slot, k]
    o_ref[...] = _rms(x1_ref[...] + moe, gfin_ref[...])


def combine(dest_flat, x1, gates, g_fin, y, *, tm=256):
    T, D = x1.shape
    n = T // tm
    blk = tm * TOP_K
    return pl.pallas_call(
        functools.partial(_combine_kernel, tm=tm),
        grid=(n,),
        in_specs=[
            pl.BlockSpec((blk,), lambda i: (i,), memory_space=pltpu.SMEM),
            pl.BlockSpec((blk,), lambda i: (jnp.minimum(i + 1, n - 1),), memory_space=pltpu.SMEM),
            pl.BlockSpec((tm, D), lambda i: (i, 0)),
            pl.BlockSpec((tm, LANES), lambda i: (i, 0)),
            pl.BlockSpec((1, D), lambda i: (0, 0)),
            pl.BlockSpec(memory_space=pl.ANY),
        ],
        out_specs=pl.BlockSpec((tm, D), lambda i: (i, 0)),
        out_shape=jax.ShapeDtypeStruct((T, D), F32),
        scratch_shapes=[pltpu.VMEM((2, TOP_K, tm, D), F32), pltpu.SemaphoreType.DMA((2,))],
        compiler_params=_params(("arbitrary",)),
        name="combine",
    )(dest_flat, dest_flat, x1, gates, g_fin, y)


def _tile_plan(counts, n_tiles, tile):
    tiles_e = (counts + tile - 1) // tile
    ends = jnp.cumsum(tiles_e)
    starts = ends - tiles_e
    pstart = starts * tile
    n_used = ends[-1]
    t = jnp.arange(n_tiles, dtype=I32)
    tc = jnp.minimum(t, n_used - 1)
    expert_of = lambda tiles: jnp.sum((tiles[:, None] >= ends[None, :]).astype(I32), axis=1)
    tile_e = expert_of(tc)
    mine = tile_e[:, None] == jnp.arange(counts.shape[0], dtype=I32)[None, :]
    pick = lambda table: jnp.sum(jnp.where(mine, table[None, :], 0), axis=1)
    left = pick(counts) - (tc - pick(starts)) * tile
    tile_rows = jnp.where(t < n_used, jnp.clip(left, 0, tile), 0)
    nxt = pick(ends)
    next_e = jnp.where(nxt < n_used, expert_of(nxt), -1)
    spare = n_used + jnp.arange(counts.shape[0], dtype=I32)
    zrow = jnp.concatenate([jnp.where(tiles_e > 0, (ends - 1) * tile, -1),
                            jnp.where(spare < n_tiles, spare * tile, -1)])
    return (pstart.astype(I32), tile_e.astype(I32), tile_rows.astype(I32), next_e.astype(I32),
            n_used.reshape(1).astype(I32), zrow.astype(I32))


def kernel(x, norm_mix_g, w_in, gla_w_gate_up, gla_b_gate, gla_norm_g, lru_conv_w, lru_conv_b, lru_w_a,
           lru_b_a, lru_w_i, lru_b_i, lru_a_param, w_out, norm_ffn_g, w_router, b_router, w_gate_up,
           b_gate_up, w_down, b_down, norm_final_g):
    B, S, D = x.shape
    assert w_in.shape[0] == 1, "the final norm is fused into the single layer's combine stage"
    T = B * S
    x2 = x.reshape(T, D)
    kw = GLA_HEADS * GLA_DK
    gw = GLA_HEADS * GLA_DV
    lw = LRU_HEADS * LRU_BLOCK
    g0 = 2 * kw + 2 * gw

    w_in0 = w_in[0]
    w_main = jnp.concatenate([w_in0[:, :g0], w_in0[:, g0 + GLA_GATE_RANK:]], axis=1).astype(BF16)
    w_glow = jnp.pad(w_in0[:, g0:g0 + GLA_GATE_RANK], ((0, 0), (0, LANES - GLA_GATE_RANK))).astype(BF16)
    z, g_low = in_proj(x2, norm_mix_g[0].reshape(1, D), w_main, w_glow)

    wgu_pad = jnp.pad(gla_w_gate_up[0], ((0, LANES - GLA_GATE_RANK), (0, 0))).astype(BF16)
    gla_o = gla(z, g_low, wgu_pad, gla_b_gate[0].reshape(1, kw), gla_norm_g[0].reshape(1, gw),
                batch=B, seq=S)
    lru_o = lru(z, lru_conv_w[0], lru_conv_b[0].reshape(1, lw), lru_w_a[0].astype(BF16),
                lru_b_a[0].reshape(1, lw), lru_w_i[0].astype(BF16), lru_b_i[0].reshape(1, lw),
                lru_a_param[0].reshape(1, lw), batch=B, seq=S, lx_col=g0)

    wr_pad = jnp.pad(w_router[0], ((0, 0), (0, LANES - N_EXPERTS)))
    wr_hi = wr_pad.astype(BF16)
    wr_lo = (wr_pad - wr_hi.astype(F32)).astype(BF16)
    br_pad = jnp.pad(b_router[0], (0, LANES - N_EXPERTS)).reshape(1, LANES)
    x1, hf, idx_pad, gates, cnt = out_proj(gla_o, lru_o, x2, w_out[0].astype(BF16),
                                           norm_ffn_g[0].reshape(1, D), wr_hi, wr_lo, br_pad)

    n_tiles = (T * TOP_K) // EXPERT_TILE + N_EXPERTS
    pstart, tile_e, tile_rows, next_e, n_used, zrow = _tile_plan(
        cnt[0, :N_EXPERTS].astype(I32), n_tiles, EXPERT_TILE)
    ps_vec = jnp.pad(pstart.astype(F32), (0, LANES - N_EXPERTS)).reshape(1, LANES)
    dest_pad = rank(idx_pad, ps_vec)
    dest_flat = dest_pad[:, :TOP_K].reshape(-1)

    xs = scatter_rows(zrow, dest_flat, hf, n_tiles * EXPERT_TILE)
    E, _, F2 = w_gate_up.shape[1:]
    h = expert_up(tile_e, tile_rows, next_e, n_used, xs, w_gate_up[0], b_gate_up[0].reshape(E, 1, F2))
    y = expert_dn(tile_e, tile_rows, next_e, n_used, h, w_down[0], b_down[0].reshape(E, 1, D))
    out = combine(dest_flat, x1, gates, norm_final_g.reshape(1, D), y)
    return out.reshape(B, S, D)
```

```python
import functools

import jax
import jax.numpy as jnp
from jax import lax
from jax.experimental import pallas as pl
from jax.experimental.pallas import tpu as pltpu

F32 = jnp.float32
BF16 = jnp.bfloat16
I32 = jnp.int32

EPS = 1e-6
LANES = 128
SUBLANES = 8
MXU_N = 256
VMEM_LIMIT = 56 * 1024 * 1024

GLA_HEADS = 4
GLA_DK = 128
GLA_DV = 256
GLA_GATE_RANK = 16
GLA_GATE_NORM = 16.0
GLA_CHUNK = 64
GLA_HEADS_PER_STEP = 4
LRU_HEADS = 4
LRU_BLOCK = 256
LRU_CONV = 4
LRU_C = 8.0
N_EXPERTS = 32
TOP_K = 4
SWIGLU_LIMIT = 7.0
SWIGLU_ALPHA = 1.702

EXPERT_TILE = 512
ROW_STEP = 128
CAST_ROWS = 32


def _params(sem, vmem=VMEM_LIMIT):
    return pltpu.CompilerParams(dimension_semantics=sem, vmem_limit_bytes=vmem)


def _rms(x, g):
    ms = jnp.mean(x * x, axis=-1, keepdims=True)
    return x * lax.rsqrt(ms + EPS) * g


def _split3(x):
    a1 = x.astype(BF16)
    r1 = x - a1.astype(F32)
    a2 = r1.astype(BF16)
    a3 = (r1 - a2.astype(F32)).astype(BF16)
    return a1, a2, a3


def _in_proj_kernel(x_ref, g_ref, w_ref, wgl_ref, z_ref, gl_ref, h_scr):
    @pl.when(pl.program_id(1) == 0)
    def _():
        h = _rms(x_ref[...], g_ref[...]).astype(BF16)
        h_scr[...] = h
        gl_ref[...] = jnp.dot(h, wgl_ref[...], preferred_element_type=F32)

    z_ref[...] = jnp.dot(h_scr[...], w_ref[...], preferred_element_type=F32).astype(z_ref.dtype)


def in_proj(x2, g, w_main, w_glow, *, tm=1024, tn=1024):
    T, D = x2.shape
    N = w_main.shape[1]
    return pl.pallas_call(
        _in_proj_kernel,
        grid=(T // tm, N // tn),
        in_specs=[
            pl.BlockSpec((tm, D), lambda i, j: (i, 0)),
            pl.BlockSpec((1, D), lambda i, j: (0, 0)),
            pl.BlockSpec((D, tn), lambda i, j: (0, j)),
            pl.BlockSpec((D, LANES), lambda i, j: (0, 0)),
        ],
        out_specs=[
            pl.BlockSpec((tm, tn), lambda i, j: (i, j)),
            pl.BlockSpec((tm, LANES), lambda i, j: (i, 0)),
        ],
        out_shape=[jax.ShapeDtypeStruct((T, N), BF16), jax.ShapeDtypeStruct((T, LANES), F32)],
        scratch_shapes=[pltpu.VMEM((tm, D), BF16)],
        compiler_params=_params(("parallel", "arbitrary")),
        name="in_proj",
    )(x2, g, w_main, w_glow)


def _gla_kernel(q_ref, k_ref, v_ref, r_ref, gl_ref, wgu_ref, bg_ref, ng_ref, o_ref, st_scr, *, tc):
    @pl.when(pl.program_id(2) == 0)
    def _():
        st_scr[...] = jnp.zeros_like(st_scr)

    ri = lax.broadcasted_iota(I32, (tc, tc), 0)
    ci = lax.broadcasted_iota(I32, (tc, tc), 1)
    same = (ri // GLA_CHUNK) == (ci // GLA_CHUNK)
    cum_m = jnp.where(same & (ci <= ri), 1.0, 0.0).astype(BF16)
    tri = (lax.broadcasted_iota(I32, (GLA_CHUNK, GLA_CHUNK), 1)
           <= lax.broadcasted_iota(I32, (GLA_CHUNK, GLA_CHUNK), 0))
    scale = GLA_DK ** -0.5
    nt = (((1,), (1,)), ((), ()))
    tn = (((0,), (0,)), ((), ()))
    chunks = range(tc // GLA_CHUNK)
    rows = [slice(c * GLA_CHUNK, (c + 1) * GLA_CHUNK) for c in chunks]
    pre = jnp.dot(gl_ref[...].astype(BF16), wgu_ref[...], preferred_element_type=F32) + bg_ref[...]
    log_g = jax.nn.log_sigmoid(pre) * (1.0 / GLA_GATE_NORM)
    bcum_all = None
    for piece in _split3(log_g):
        part = jnp.dot(cum_m, piece, preferred_element_type=F32)
        bcum_all = part if bcum_all is None else bcum_all + part

    for hd in range(GLA_HEADS_PER_STEP):
        kc = slice(hd * GLA_DK, (hd + 1) * GLA_DK)
        vc = slice(hd * GLA_DV, (hd + 1) * GLA_DV)
        qe, ke, kd, dec = [], [], [], []
        for sl in rows:
            b = bcum_all[sl, kc]
            bl = b[GLA_CHUNK - 1:GLA_CHUNK]
            qf = q_ref[sl, kc].astype(F32) * scale
            kf = k_ref[sl, kc].astype(F32)
            qe.append((qf * jnp.exp(b)).astype(BF16))
            ke.append((kf * jnp.exp(-b)).astype(BF16))
            kd.append((kf * jnp.exp(bl - b)).astype(BF16))
            dec.append(jnp.exp(bl))
        scores = [jnp.where(tri, lax.dot_general(qe[c], ke[c], nt, preferred_element_type=F32), 0.0).astype(BF16)
                  for c in chunks]
        st = st_scr[hd]
        outs = []
        for c in chunks:
            vb = v_ref[rows[c], vc]
            o = jnp.dot(scores[c], vb, preferred_element_type=F32)
            o = o + lax.dot_general(qe[c], st.astype(BF16), nt, preferred_element_type=F32)
            st = st * dec[c] + lax.dot_general(vb, kd[c], tn, preferred_element_type=F32)
            outs.append(o)
        st_scr[hd] = st
        o = jnp.concatenate(outs, axis=0)
        on = _rms(o, ng_ref[:, vc])
        r = r_ref[:, vc].astype(F32)
        o_ref[:, vc] = (jax.nn.silu(r) * on).astype(o_ref.dtype)


def gla(z, g_low, wgu_pad, b_gate, norm_g, *, batch, seq, tc=256):
    T = z.shape[0]
    nc = seq // tc
    hp = GLA_HEADS_PER_STEP
    dk, dv = hp * GLA_DK, hp * GLA_DV
    kw = GLA_HEADS * GLA_DK
    k0 = kw // dk
    v0 = (2 * kw) // dv
    r0 = v0 + GLA_HEADS // hp
    row = lambda b, h, c: b * nc + c
    return pl.pallas_call(
        functools.partial(_gla_kernel, tc=tc),
        grid=(batch, GLA_HEADS // hp, nc),
        in_specs=[
            pl.BlockSpec((tc, dk), lambda b, h, c: (row(b, h, c), h)),
            pl.BlockSpec((tc, dk), lambda b, h, c: (row(b, h, c), k0 + h)),
            pl.BlockSpec((tc, dv), lambda b, h, c: (row(b, h, c), v0 + h)),
            pl.BlockSpec((tc, dv), lambda b, h, c: (row(b, h, c), r0 + h)),
            pl.BlockSpec((tc, LANES), lambda b, h, c: (row(b, h, c), 0)),
            pl.BlockSpec((LANES, dk), lambda b, h, c: (0, h)),
            pl.BlockSpec((1, dk), lambda b, h, c: (0, h)),
            pl.BlockSpec((1, dv), lambda b, h, c: (0, h)),
        ],
        out_specs=pl.BlockSpec((tc, dv), lambda b, h, c: (row(b, h, c), h)),
        out_shape=jax.ShapeDtypeStruct((T, GLA_HEADS * GLA_DV), BF16),
        scratch_shapes=[pltpu.VMEM((hp, GLA_DV, GLA_DK), F32)],
        compiler_params=_params(("parallel", "parallel", "arbitrary")),
        name="gla",
    )(z, z, z, z, g_low, wgu_pad, b_gate, norm_g)


def _lru_kernel(lx_ref, lg_ref, cw_ref, cb_ref, wa_ref, ba_ref, wi_ref, bi_ref, ap_ref, o_ref,
                tail_scr, h_scr, a_scr, u_scr, hs_scr, *, tc):
    @pl.when(pl.program_id(2) == 0)
    def _():
        tail_scr[...] = jnp.zeros_like(tail_scr)
        h_scr[...] = jnp.zeros_like(h_scr)

    x = lx_ref[...].astype(F32)
    tail_scr[SUBLANES:, :] = x
    xc = cb_ref[...]
    for i in range(LRU_CONV):
        s0 = SUBLANES - (LRU_CONV - 1) + i
        xc = xc + tail_scr[s0:s0 + tc, :] * cw_ref[i:i + 1, :]
    tail_scr[:SUBLANES, :] = x[tc - SUBLANES:, :]

    xb = xc.astype(BF16)
    ga = jax.nn.sigmoid(jnp.dot(xb, wa_ref[0], preferred_element_type=F32) + ba_ref[...])
    gi = jax.nn.sigmoid(jnp.dot(xb, wi_ref[0], preferred_element_type=F32) + bi_ref[...])
    log_a = -LRU_C * ga * jax.nn.softplus(-ap_ref[...])
    a = jnp.exp(log_a)
    u = xc * gi * jnp.sqrt(-jnp.tanh(log_a) * (a * a + 1.0))

    grouped = (tc // SUBLANES, SUBLANES, a.shape[1])
    a = a.reshape(grouped)
    u = u.reshape(grouped)
    sub = lax.broadcasted_iota(I32, grouped, 1)
    for s in (1, 2, 4):
        keep = sub >= s
        a_sh = jnp.where(keep, pltpu.roll(a, s, axis=1), 1.0)
        u_sh = jnp.where(keep, pltpu.roll(u, s, axis=1), 0.0)
        u = u + a * u_sh
        a = a * a_sh
    a_scr[...] = a.reshape(tc, grouped[2])
    u_scr[...] = u.reshape(tc, grouped[2])

    def body(g, hp):
        rows = pl.ds(pl.multiple_of(g * SUBLANES, SUBLANES), SUBLANES)
        hr = a_scr[rows, :] * hp + u_scr[rows, :]
        hs_scr[rows, :] = hr
        return jnp.broadcast_to(hr[SUBLANES - 1:, :], hr.shape)

    h_scr[...] = lax.fori_loop(0, tc // SUBLANES, body, h_scr[...], unroll=8)
    gate = jax.nn.gelu(lg_ref[...].astype(F32))
    o_ref[...] = (hs_scr[...] * gate).astype(o_ref.dtype)


def lru(z, conv_w, conv_b, w_a, b_a, w_i, b_i, a_param, *, batch, seq, lx_col, tc=512):
    T = z.shape[0]
    nc = seq // tc
    W = LRU_BLOCK
    lx0 = lx_col // W
    lg0 = lx0 + LRU_HEADS
    row = lambda b, h, c: b * nc + c
    vec = pl.BlockSpec((1, W), lambda b, h, c: (0, h))
    mat = pl.BlockSpec((1, W, W), lambda b, h, c: (h, 0, 0))
    return pl.pallas_call(
        functools.partial(_lru_kernel, tc=tc),
        grid=(batch, LRU_HEADS, nc),
        in_specs=[
            pl.BlockSpec((tc, W), lambda b, h, c: (row(b, h, c), lx0 + h)),
            pl.BlockSpec((tc, W), lambda b, h, c: (row(b, h, c), lg0 + h)),
            pl.BlockSpec((LRU_CONV, W), lambda b, h, c: (0, h)),
            vec, mat, vec, mat, vec, vec,
        ],
        out_specs=pl.BlockSpec((tc, W), lambda b, h, c: (row(b, h, c), h)),
        out_shape=jax.ShapeDtypeStruct((T, LRU_HEADS * W), BF16),
        scratch_shapes=[
            pltpu.VMEM((SUBLANES + tc, W), F32),
            pltpu.VMEM((SUBLANES, W), F32),
            pltpu.VMEM((tc, W), F32),
            pltpu.VMEM((tc, W), F32),
            pltpu.VMEM((tc, W), F32),
        ],
        compiler_params=_params(("parallel", "parallel", "arbitrary")),
        name="lru",
    )(z, z, conv_w, conv_b, w_a, b_a, w_i, b_i, a_param)


def _out_proj_kernel(go_ref, lo_ref, x_ref, wo_ref, gf_ref, wrh_ref, wrl_ref, br_ref,
                     x1_ref, hf_ref, idx_ref, gate_ref, cnt_ref):
    @pl.when(pl.program_id(0) == 0)
    def _():
        cnt_ref[...] = jnp.zeros_like(cnt_ref)

    half = go_ref.shape[1]
    mix = jnp.dot(go_ref[...], wo_ref[:half, :], preferred_element_type=F32)
    mix = mix + jnp.dot(lo_ref[...], wo_ref[half:, :], preferred_element_type=F32)
    x1 = x_ref[...] + mix
    x1_ref[...] = x1
    hf = _rms(x1, gf_ref[...])
    hf_ref[...] = hf
    hf_hi = hf.astype(BF16)
    hf_lo = (hf - hf_hi.astype(F32)).astype(BF16)

    logits = jnp.dot(hf_hi, wrh_ref[...], preferred_element_type=F32)
    logits = logits + jnp.dot(hf_lo, wrh_ref[...], preferred_element_type=F32)
    logits = logits + jnp.dot(hf_hi, wrl_ref[...], preferred_element_type=F32) + br_ref[...]
    lane = lax.broadcasted_iota(I32, logits.shape, 1)
    lanef = lane.astype(F32)
    neg = jnp.float32(-jnp.inf)
    work = jnp.where(lane < N_EXPERTS, logits, neg)
    vals, idxs = [], []
    hot = jnp.zeros(logits.shape, F32)
    for _ in range(TOP_K):
        m = jnp.max(work, axis=-1, keepdims=True)
        sel = jnp.min(jnp.where(work == m, lanef, float(LANES)), axis=-1, keepdims=True)
        picked = lanef == sel
        work = jnp.where(picked, neg, work)
        hot = jnp.where(picked, 1.0, hot)
        vals.append(m)
        idxs.append(sel)
    ex = [jnp.exp(v - vals[0]) for v in vals]
    den = ex[0] + ex[1] + ex[2] + ex[3]
    idx_out = jnp.zeros(logits.shape, F32)
    gate_out = jnp.zeros(logits.shape, F32)
    for k in range(TOP_K):
        idx_out = jnp.where(lane == k, idxs[k], idx_out)
        gate_out = jnp.where(lane == k, ex[k] / den, gate_out)
    idx_ref[...] = idx_out.astype(I32)
    gate_ref[...] = gate_out
    cnt_ref[...] += jnp.sum(hot, axis=0, keepdims=True)


def out_proj(gla_o, lru_o, x2, w_out, g_ffn, wr_hi, wr_lo, br_pad, *, tm=512):
    T, D = x2.shape
    half = gla_o.shape[1]
    row = lambda i: (i, 0)
    fixed = lambda i: (0, 0)
    return pl.pallas_call(
        _out_proj_kernel,
        grid=(T // tm,),
        in_specs=[
            pl.BlockSpec((tm, half), row),
            pl.BlockSpec((tm, half), row),
            pl.BlockSpec((tm, D), row),
            pl.BlockSpec((2 * half, D), fixed),
            pl.BlockSpec((1, D), fixed),
            pl.BlockSpec((D, LANES), fixed),
            pl.BlockSpec((D, LANES), fixed),
            pl.BlockSpec((1, LANES), fixed),
        ],
        out_specs=[
            pl.BlockSpec((tm, D), row),
            pl.BlockSpec((tm, D), row),
            pl.BlockSpec((tm, LANES), row),
            pl.BlockSpec((tm, LANES), row),
            pl.BlockSpec((SUBLANES, LANES), fixed),
        ],
        out_shape=[
            jax.ShapeDtypeStruct((T, D), F32),
            jax.ShapeDtypeStruct((T, D), F32),
            jax.ShapeDtypeStruct((T, LANES), I32),
            jax.ShapeDtypeStruct((T, LANES), F32),
            jax.ShapeDtypeStruct((SUBLANES, LANES), F32),
        ],
        compiler_params=_params(("arbitrary",)),
        name="out_proj",
    )(gla_o, lru_o, x2, w_out, g_ffn, wr_hi, wr_lo, br_pad)


def _rank_kernel(idx_ref, ps_ref, dest_ref, carry_scr):
    @pl.when(pl.program_id(0) == 0)
    def _():
        carry_scr[...] = jnp.zeros_like(carry_scr)

    idx = idx_ref[...]
    tm = idx.shape[0]
    lane = lax.broadcasted_iota(I32, idx.shape, 1)
    hot = [idx[:, k:k + 1] == lane for k in range(TOP_K)]
    oh = jnp.zeros(idx.shape, F32)
    for k in range(TOP_K):
        oh = oh + jnp.where(hot[k], 1.0, 0.0)
    lower = (lax.broadcasted_iota(I32, (tm, tm), 1) < lax.broadcasted_iota(I32, (tm, tm), 0))
    before = jnp.dot(jnp.where(lower, 1.0, 0.0).astype(BF16), oh.astype(BF16),
                     preferred_element_type=F32)
    base = before + (carry_scr[0:1, :] + ps_ref[...])
    out = jnp.zeros(idx.shape, F32)
    for k in range(TOP_K):
        dk = jnp.sum(jnp.where(hot[k], base, 0.0), axis=-1, keepdims=True)
        out = jnp.where(lane == k, dk, out)
    dest_ref[...] = out.astype(I32)
    carry_scr[...] += jnp.sum(oh, axis=0, keepdims=True)


def rank(idx_pad, ps_vec, *, tm=512):
    T = idx_pad.shape[0]
    return pl.pallas_call(
        _rank_kernel,
        grid=(T // tm,),
        in_specs=[
            pl.BlockSpec((tm, LANES), lambda i: (i, 0)),
            pl.BlockSpec((1, LANES), lambda i: (0, 0)),
        ],
        out_specs=pl.BlockSpec((tm, LANES), lambda i: (i, 0)),
        out_shape=jax.ShapeDtypeStruct((T, LANES), I32),
        scratch_shapes=[pltpu.VMEM((SUBLANES, LANES), F32)],
        compiler_params=_params(("arbitrary",)),
        name="rank",
    )(idx_pad, ps_vec)


def _scatter_kernel(zrow_ref, dest_ref, hp_ref, xs_ref, zero_scr, sem, zsem, *, tm):
    @pl.when(pl.program_id(0) == 0)
    def _():
        zero_scr[...] = jnp.zeros_like(zero_scr)

        def zero_copy(e):
            r0 = pl.multiple_of(zrow_ref[e], SUBLANES)
            return pltpu.make_async_copy(zero_scr, xs_ref.at[pl.ds(r0, EXPERT_TILE)], zsem)

        def zstart(e, carry):
            @pl.when(zrow_ref[e] >= 0)
            def _():
                zero_copy(e).start()
            return carry

        def zwait(e, carry):
            @pl.when(zrow_ref[e] >= 0)
            def _():
                zero_copy(e).wait()
            return carry

        lax.fori_loop(0, zrow_ref.shape[0], zstart, 0)
        lax.fori_loop(0, zrow_ref.shape[0], zwait, 0)

    def issue(t, carry):
        for k in range(TOP_K):
            d = dest_ref[t * TOP_K + k]
            pltpu.make_async_copy(hp_ref.at[pl.ds(t, 1)], xs_ref.at[pl.ds(d, 1)], sem).start(priority=k % 2)
        return carry

    for t in range(tm):
        issue(t, 0)
    for k in range(TOP_K):
        pltpu.make_async_copy(hp_ref, xs_ref.at[pl.ds(0, tm)], sem).wait()


def scatter_rows(zrow, dest_flat, hp, n_rows, *, tm=512):
    T, W = hp.shape
    return pl.pallas_call(
        functools.partial(_scatter_kernel, tm=tm),
        grid_spec=pltpu.PrefetchScalarGridSpec(
            num_scalar_prefetch=1,
            grid=(T // tm,),
            in_specs=[
                pl.BlockSpec((tm * TOP_K,), lambda i, zr: (i,), memory_space=pltpu.SMEM),
                pl.BlockSpec((tm, W), lambda i, zr: (i, 0)),
            ],
            out_specs=pl.BlockSpec(memory_space=pl.ANY),
            scratch_shapes=[
                pltpu.VMEM((EXPERT_TILE, W), hp.dtype),
                pltpu.SemaphoreType.DMA(()),
                pltpu.SemaphoreType.DMA(()),
            ],
        ),
        out_shape=jax.ShapeDtypeStruct((n_rows, W), hp.dtype),
        compiler_params=_params(("arbitrary",)),
        name="scatter_rows",
    )(zrow, dest_flat, hp)


def _tile_changed(te_ref, i):
    return (i == 0) | (te_ref[i] != te_ref[jnp.maximum(i - 1, 0)])


def _stream_expert_weights(te_ref, tr_ref, ne_ref, fetch, cast, nj):
    j = pl.program_id(0)
    i = pl.program_id(1)

    @pl.when((tr_ref[i] > 0) & _tile_changed(te_ref, i))
    def _():
        @pl.when((j == 0) & (i == 0))
        def _():
            for c in fetch(te_ref[0], 0):
                c.start()

        for c in fetch(te_ref[i], j):
            c.wait()
        cast()
        wrap = ne_ref[i] < 0
        e_next = jnp.where(wrap, te_ref[0], ne_ref[i])
        j_next = jnp.where(wrap, j + 1, j)

        @pl.when(j_next < nj)
        def _():
            for c in fetch(e_next, j_next):
                c.start()


def _for_row_count(rows, out_ref, compute):
    for m in range(ROW_STEP, EXPERT_TILE + 1, ROW_STEP):
        @pl.when((rows > m - ROW_STEP) & (rows <= m))
        def _(m=m):
            compute(m)

    @pl.when(rows == 0)
    def _():
        out_ref[...] = jnp.zeros_like(out_ref)


def _swiglu_rows(xs_ref, wg_scr, wu_scr, bg_ref, bu_ref, h_ref, m):
    xb = xs_ref[:m, :].astype(BF16)
    tn = h_ref.shape[1]
    for n0 in range(0, tn, MXU_N):
        cols = slice(n0, n0 + MXU_N)

        def proj(w_scr, b_ref):
            return jnp.dot(xb, w_scr[:, cols], preferred_element_type=F32) + b_ref[0, :, cols]

        gate = jnp.minimum(proj(wg_scr, bg_ref), SWIGLU_LIMIT)
        up = jnp.clip(proj(wu_scr, bu_ref), -SWIGLU_LIMIT, SWIGLU_LIMIT)
        glu = gate * jax.nn.sigmoid(SWIGLU_ALPHA * gate)
        h_ref[:m, cols] = ((up + 1.0) * glu).astype(h_ref.dtype)
    if m < h_ref.shape[0]:
        h_ref[m:, :] = jnp.zeros((h_ref.shape[0] - m, tn), h_ref.dtype)


def _expert_up_kernel(te_ref, tr_ref, ne_ref, nu_ref, xs_ref, w_hbm, bg_ref, bu_ref, h_ref,
                      stage, wg_scr, wu_scr, sems, *, nj):
    tn = h_ref.shape[1]
    F = nj * tn

    def fetch(e, j):
        c0 = pl.multiple_of(j * tn, tn)
        c1 = pl.multiple_of(F + j * tn, tn)
        return (pltpu.make_async_copy(w_hbm.at[e, :, pl.ds(c0, tn)], stage.at[0], sems.at[0]),
                pltpu.make_async_copy(w_hbm.at[e, :, pl.ds(c1, tn)], stage.at[1], sems.at[1]))

    def cast():
        def chunk(c, carry):
            r = pl.ds(pl.multiple_of(c * CAST_ROWS, CAST_ROWS), CAST_ROWS)
            wg_scr[r, :] = stage[0, r, :].astype(BF16)
            wu_scr[r, :] = stage[1, r, :].astype(BF16)
            return carry

        lax.fori_loop(0, wg_scr.shape[0] // CAST_ROWS, chunk, 0)

    _stream_expert_weights(te_ref, tr_ref, ne_ref, fetch, cast, nj)
    rows = tr_ref[pl.program_id(1)]
    _for_row_count(rows, h_ref, functools.partial(_swiglu_rows, xs_ref, wg_scr, wu_scr, bg_ref, bu_ref, h_ref))


def expert_up(tile_e, tile_rows, next_e, n_used, xs, w_gate_up, b_gate_up3, *, tm=EXPERT_TILE, tn=1024):
    P, D = xs.shape
    F = w_gate_up.shape[2] // 2
    nj = F // tn
    row = lambda i, nu: jnp.minimum(i, nu[0] - 1)
    return pl.pallas_call(
        functools.partial(_expert_up_kernel, nj=nj),
        grid_spec=pltpu.PrefetchScalarGridSpec(
            num_scalar_prefetch=4,
            grid=(nj, P // tm),
            in_specs=[
                pl.BlockSpec((tm, D), lambda j, i, te, tr, ne, nu: (row(i, nu), 0)),
                pl.BlockSpec(memory_space=pl.ANY),
                pl.BlockSpec((1, 1, tn), lambda j, i, te, tr, ne, nu: (te[i], 0, j)),
                pl.BlockSpec((1, 1, tn), lambda j, i, te, tr, ne, nu: (te[i], 0, nj + j)),
            ],
            out_specs=pl.BlockSpec((tm, tn), lambda j, i, te, tr, ne, nu: (i, j)),
            scratch_shapes=[
                pltpu.VMEM((2, D, tn), F32),
                pltpu.VMEM((D, tn), BF16),
                pltpu.VMEM((D, tn), BF16),
                pltpu.SemaphoreType.DMA((2,)),
            ],
        ),
        out_shape=jax.ShapeDtypeStruct((P, F), BF16),
        compiler_params=_params(("arbitrary", "arbitrary")),
        name="expert_up",
    )(tile_e, tile_rows, next_e, n_used, xs, w_gate_up, b_gate_up3, b_gate_up3)


def _down_rows(h_ref, wd_scr, bd_ref, y_ref, m):
    y_ref[:m, :] = jnp.dot(h_ref[:m, :], wd_scr[...], preferred_element_type=F32) + bd_ref[0]
    if m < y_ref.shape[0]:
        y_ref[m:, :] = jnp.zeros((y_ref.shape[0] - m, y_ref.shape[1]), y_ref.dtype)


def _expert_dn_kernel(te_ref, tr_ref, ne_ref, nu_ref, h_ref, w_hbm, bd_ref, y_ref, stage, wd_scr, sem, *, nj):
    tn = y_ref.shape[1]

    def fetch(e, j):
        c0 = pl.multiple_of(j * tn, tn)
        return (pltpu.make_async_copy(w_hbm.at[e, :, pl.ds(c0, tn)], stage, sem),)

    def cast():
        def chunk(c, carry):
            r = pl.ds(pl.multiple_of(c * CAST_ROWS, CAST_ROWS), CAST_ROWS)
            wd_scr[r, :] = stage[r, :].astype(BF16)
            return carry

        lax.fori_loop(0, wd_scr.shape[0] // CAST_ROWS, chunk, 0)

    _stream_expert_weights(te_ref, tr_ref, ne_ref, fetch, cast, nj)
    rows = tr_ref[pl.program_id(1)]
    _for_row_count(rows, y_ref, functools.partial(_down_rows, h_ref, wd_scr, bd_ref, y_ref))


def expert_dn(tile_e, tile_rows, next_e, n_used, h, w_down, b_down3, *, tm=EXPERT_TILE, tn=2048):
    P, F = h.shape
    D = w_down.shape[2]
    nj = D // tn
    row = lambda i, nu: jnp.minimum(i, nu[0] - 1)
    return pl.pallas_call(
        functools.partial(_expert_dn_kernel, nj=nj),
        grid_spec=pltpu.PrefetchScalarGridSpec(
            num_scalar_prefetch=4,
            grid=(nj, P // tm),
            in_specs=[
                pl.BlockSpec((tm, F), lambda j, i, te, tr, ne, nu: (row(i, nu), 0)),
                pl.BlockSpec(memory_space=pl.ANY),
                pl.BlockSpec((1, 1, tn), lambda j, i, te, tr, ne, nu: (te[i], 0, j)),
            ],
            out_specs=pl.BlockSpec((tm, tn), lambda j, i, te, tr, ne, nu: (i, j)),
            scratch_shapes=[
                pltpu.VMEM((F, tn), F32),
                pltpu.VMEM((F, tn), BF16),
                pltpu.SemaphoreType.DMA(()),
            ],
        ),
        out_shape=jax.ShapeDtypeStruct((P, D), F32),
        compiler_params=_params(("arbitrary", "arbitrary")),
        name="expert_dn",
    )(tile_e, tile_rows, next_e, n_used, h, w_down, b_down3)


def _combine_kernel(dcur_ref, dnxt_ref, x1_ref, gate_ref, gfin_ref, y_ref, o_ref, buf, sems, *, tm):
    i = pl.program_id(0)
    n = pl.num_programs(0)
    slot = i % 2

    def gather(dest_ref, s, unrolled):
        def issue(t, carry):
            for k in range(TOP_K):
                d = dest_ref[t * TOP_K + k]
                pltpu.make_async_copy(y_ref.at[pl.ds(d, 1)], buf.at[s, k, pl.ds(t, 1)],
                                      sems.at[s]).start(priority=k % 2)
            return carry

        if unrolled:
            for t in range(tm):
                issue(t, 0)
        else:
            lax.fori_loop(0, tm, issue, 0, unroll=8)

    @pl.when(i == 0)
    def _():
        gather(dcur_ref, 0, False)

    @pl.when(i + 1 < n)
    def _():
        gather(dnxt_ref, 1 - slot, True)

    for k in range(TOP_K):
        pltpu.make_async_copy(y_ref.at[pl.ds(0, tm)], buf.at[slot, k], sems.at[slot]).wait()

    gates = gate_ref[...]
    moe = gates[:, 0:1] * buf[slot, 0]
    for k in range(1, TOP_K):
        moe = moe + gates[:, k:k + 1] * buf[slot, k]
    o_ref[...] = _rms(x1_ref[...] + moe, gfin_ref[...])


def combine(dest_flat, x1, gates, g_fin, y, *, tm=256):
    T, D = x1.shape
    n = T // tm
    blk = tm * TOP_K
    return pl.pallas_call(
        functools.partial(_combine_kernel, tm=tm),
        grid=(n,),
        in_specs=[
            pl.BlockSpec((blk,), lambda i: (i,), memory_space=pltpu.SMEM),
            pl.BlockSpec((blk,), lambda i: (jnp.minimum(i + 1, n - 1),), memory_space=pltpu.SMEM),
            pl.BlockSpec((tm, D), lambda i: (i, 0)),
            pl.BlockSpec((tm, LANES), lambda i: (i, 0)),
            pl.BlockSpec((1, D), lambda i: (0, 0)),
            pl.BlockSpec(memory_space=pl.ANY),
        ],
        out_specs=pl.BlockSpec((tm, D), lambda i: (i, 0)),
        out_shape=jax.ShapeDtypeStruct((T, D), F32),
        scratch_shapes=[pltpu.VMEM((2, TOP_K, tm, D), F32), pltpu.SemaphoreType.DMA((2,))],
        compiler_params=_params(("arbitrary",)),
        name="combine",
    )(dest_flat, dest_flat, x1, gates, g_fin, y)


def _tile_plan(counts, n_tiles, tile):
    tiles_e = (counts + tile - 1) // tile
    ends = jnp.cumsum(tiles_e)
    starts = ends - tiles_e
    pstart = starts * tile
    n_used = ends[-1]
    t = jnp.arange(n_tiles, dtype=I32)
    tc = jnp.minimum(t, n_used - 1)
    expert_of = lambda tiles: jnp.sum((tiles[:, None] >= ends[None, :]).astype(I32), axis=1)
    tile_e = expert_of(tc)
    mine = tile_e[:, None] == jnp.arange(counts.shape[0], dtype=I32)[None, :]
    pick = lambda table: jnp.sum(jnp.where(mine, table[None, :], 0), axis=1)
    left = pick(counts) - (tc - pick(starts)) * tile
    tile_rows = jnp.where(t < n_used, jnp.clip(left, 0, tile), 0)
    nxt = pick(ends)
    next_e = jnp.where(nxt < n_used, expert_of(nxt), -1)
    spare = n_used + jnp.arange(counts.shape[0], dtype=I32)
    zrow = jnp.concatenate([jnp.where(tiles_e > 0, (ends - 1) * tile, -1),
                            jnp.where(spare < n_tiles, spare * tile, -1)])
    return (pstart.astype(I32), tile_e.astype(I32), tile_rows.astype(I32), next_e.astype(I32),
            n_used.reshape(1).astype(I32), zrow.astype(I32))


def kernel(x, norm_mix_g, w_in, gla_w_gate_up, gla_b_gate, gla_norm_g, lru_conv_w, lru_conv_b, lru_w_a,
           lru_b_a, lru_w_i, lru_b_i, lru_a_param, w_out, norm_ffn_g, w_router, b_router, w_gate_up,
           b_gate_up, w_down, b_down, norm_final_g):
    B, S, D = x.shape
    assert w_in.shape[0] == 1, "the final norm is fused into the single layer's combine stage"
    T = B * S
    x2 = x.reshape(T, D)
    kw = GLA_HEADS * GLA_DK
    gw = GLA_HEADS * GLA_DV
    lw = LRU_HEADS * LRU_BLOCK
    g0 = 2 * kw + 2 * gw

    w_in0 = w_in[0]
    w_main = jnp.concatenate([w_in0[:, :g0], w_in0[:, g0 + GLA_GATE_RANK:]], axis=1).astype(BF16)
    w_glow = jnp.pad(w_in0[:, g0:g0 + GLA_GATE_RANK], ((0, 0), (0, LANES - GLA_GATE_RANK))).astype(BF16)
    z, g_low = in_proj(x2, norm_mix_g[0].reshape(1, D), w_main, w_glow)

    wgu_pad = jnp.pad(gla_w_gate_up[0], ((0, LANES - GLA_GATE_RANK), (0, 0))).astype(BF16)
    gla_o = gla(z, g_low, wgu_pad, gla_b_gate[0].reshape(1, kw), gla_norm_g[0].reshape(1, gw),
                batch=B, seq=S)
    lru_o = lru(z, lru_conv_w[0], lru_conv_b[0].reshape(1, lw), lru_w_a[0].astype(BF16),
                lru_b_a[0].reshape(1, lw), lru_w_i[0].astype(BF16), lru_b_i[0].reshape(1, lw),
                lru_a_param[0].reshape(1, lw), batch=B, seq=S, lx_col=g0)

    wr_pad = jnp.pad(w_router[0], ((0, 0), (0, LANES - N_EXPERTS)))
    wr_hi = wr_pad.astype(BF16)
    wr_lo = (wr_pad - wr_hi.astype(F32)).astype(BF16)
    br_pad = jnp.pad(b_router[0], (0, LANES - N_EXPERTS)).reshape(1, LANES)
    x1, hf, idx_pad, gates, cnt = out_proj(gla_o, lru_o, x2, w_out[0].astype(BF16),
                                           norm_ffn_g[0].reshape(1, D), wr_hi, wr_lo, br_pad)

    n_tiles = (T * TOP_K) // EXPERT_TILE + N_EXPERTS
    pstart, tile_e, tile_rows, next_e, n_used, zrow = _tile_plan(
        cnt[0, :N_EXPERTS].astype(I32), n_tiles, EXPERT_TILE)
    ps_vec = jnp.pad(pstart.astype(F32), (0, LANES - N_EXPERTS)).reshape(1, LANES)
    dest_pad = rank(idx_pad, ps_vec)
    dest_flat = dest_pad[:, :TOP_K].reshape(-1)

    xs = scatter_rows(zrow, dest_flat, hf, n_tiles * EXPERT_TILE)
    E, _, F2 = w_gate_up.shape[1:]
    h = expert_up(tile_e, tile_rows, next_e, n_used, xs, w_gate_up[0], b_gate_up[0].reshape(E, 1, F2))
    y = expert_dn(tile_e, tile_rows, next_e, n_used, h, w_down[0], b_down[0].reshape(E, 1, D))
    out = combine(dest_flat, x1, gates, norm_final_g.reshape(1, D), y)
    return out.reshape(B, S, D)
```

```python
import functools

import jax
import jax.numpy as jnp
from jax import lax
from jax.experimental import pallas as pl
from jax.experimental.pallas import tpu as pltpu

F32 = jnp.float32
BF16 = jnp.bfloat16
I32 = jnp.int32

EPS = 1e-6
LANES = 128
SUBLANES = 8
MXU_N = 256
VMEM_LIMIT = 56 * 1024 * 1024

GLA_HEADS = 4
GLA_DK = 128
GLA_DV = 256
GLA_GATE_RANK = 16
GLA_GATE_NORM = 16.0
GLA_CHUNK = 64
GLA_HEADS_PER_STEP = 4
LRU_HEADS = 4
LRU_BLOCK = 256
LRU_CONV = 4
LRU_C = 8.0
N_EXPERTS = 32
TOP_K = 4
SWIGLU_LIMIT = 7.0
SWIGLU_ALPHA = 1.702

EXPERT_TILE = 512
ROW_STEP = 128
CAST_ROWS = 32


def _params(sem, vmem=VMEM_LIMIT):
    return pltpu.CompilerParams(dimension_semantics=sem, vmem_limit_bytes=vmem)


def _rms(x, g):
    ms = jnp.mean(x * x, axis=-1, keepdims=True)
    return x * lax.rsqrt(ms + EPS) * g


def _split3(x):
    a1 = x.astype(BF16)
    r1 = x - a1.astype(F32)
    a2 = r1.astype(BF16)
    a3 = (r1 - a2.astype(F32)).astype(BF16)
    return a1, a2, a3


def _in_proj_kernel(x_ref, g_ref, w_ref, wgl_ref, z_ref, gl_ref, h_scr):
    @pl.when(pl.program_id(1) == 0)
    def _():
        h = _rms(x_ref[...], g_ref[...]).astype(BF16)
        h_scr[...] = h
        gl_ref[...] = jnp.dot(h, wgl_ref[...], preferred_element_type=F32)

    z_ref[...] = jnp.dot(h_scr[...], w_ref[...], preferred_element_type=F32).astype(z_ref.dtype)


def in_proj(x2, g, w_main, w_glow, *, tm=1024, tn=1024):
    T, D = x2.shape
    N = w_main.shape[1]
    return pl.pallas_call(
        _in_proj_kernel,
        grid=(T // tm, N // tn),
        in_specs=[
            pl.BlockSpec((tm, D), lambda i, j: (i, 0)),
            pl.BlockSpec((1, D), lambda i, j: (0, 0)),
            pl.BlockSpec((D, tn), lambda i, j: (0, j)),
            pl.BlockSpec((D, LANES), lambda i, j: (0, 0)),
        ],
        out_specs=[
            pl.BlockSpec((tm, tn), lambda i, j: (i, j)),
            pl.BlockSpec((tm, LANES), lambda i, j: (i, 0)),
        ],
        out_shape=[jax.ShapeDtypeStruct((T, N), BF16), jax.ShapeDtypeStruct((T, LANES), F32)],
        scratch_shapes=[pltpu.VMEM((tm, D), BF16)],
        compiler_params=_params(("parallel", "arbitrary")),
        name="in_proj",
    )(x2, g, w_main, w_glow)


def _gla_kernel(q_ref, k_ref, v_ref, r_ref, gl_ref, wgu_ref, bg_ref, ng_ref, o_ref, st_scr, *, tc):
    @pl.when(pl.program_id(2) == 0)
    def _():
        st_scr[...] = jnp.zeros_like(st_scr)

    ri = lax.broadcasted_iota(I32, (tc, tc), 0)
    ci = lax.broadcasted_iota(I32, (tc, tc), 1)
    same = (ri // GLA_CHUNK) == (ci // GLA_CHUNK)
    cum_m = jnp.where(same & (ci <= ri), 1.0, 0.0).astype(BF16)
    tri = (lax.broadcasted_iota(I32, (GLA_CHUNK, GLA_CHUNK), 1)
           <= lax.broadcasted_iota(I32, (GLA_CHUNK, GLA_CHUNK), 0))
    scale = GLA_DK ** -0.5
    nt = (((1,), (1,)), ((), ()))
    tn = (((0,), (0,)), ((), ()))
    chunks = range(tc // GLA_CHUNK)
    rows = [slice(c * GLA_CHUNK, (c + 1) * GLA_CHUNK) for c in chunks]
    pre = jnp.dot(gl_ref[...].astype(BF16), wgu_ref[...], preferred_element_type=F32) + bg_ref[...]
    log_g = jax.nn.log_sigmoid(pre) * (1.0 / GLA_GATE_NORM)
    bcum_all = None
    for piece in _split3(log_g):
        part = jnp.dot(cum_m, piece, preferred_element_type=F32)
        bcum_all = part if bcum_all is None else bcum_all + part

    for hd in range(GLA_HEADS_PER_STEP):
        kc = slice(hd * GLA_DK, (hd + 1) * GLA_DK)
        vc = slice(hd * GLA_DV, (hd + 1) * GLA_DV)
        qe, ke, kd, dec = [], [], [], []
        for sl in rows:
            b = bcum_all[sl, kc]
            bl = b[GLA_CHUNK - 1:GLA_CHUNK]
            qf = q_ref[sl, kc].astype(F32) * scale
            kf = k_ref[sl, kc].astype(F32)
            qe.append((qf * jnp.exp(b)).astype(BF16))
            ke.append((kf * jnp.exp(-b)).astype(BF16))
            kd.append((kf * jnp.exp(bl - b)).astype(BF16))
            dec.append(jnp.exp(bl))
        scores = [jnp.where(tri, lax.dot_general(qe[c], ke[c], nt, preferred_element_type=F32), 0.0).astype(BF16)
                  for c in chunks]
        st = st_scr[hd]
        outs = []
        for c in chunks:
            vb = v_ref[rows[c], vc]
            o = jnp.dot(scores[c], vb, preferred_element_type=F32)
            o = o + lax.dot_general(qe[c], st.astype(BF16), nt, preferred_element_type=F32)
            st = st * dec[c] + lax.dot_general(vb, kd[c], tn, preferred_element_type=F32)
            outs.append(o)
        st_scr[hd] = st
        o = jnp.concatenate(outs, axis=0)
        on = _rms(o, ng_ref[:, vc])
        r = r_ref[:, vc].astype(F32)
        o_ref[:, vc] = (jax.nn.silu(r) * on).astype(o_ref.dtype)


def gla(z, g_low, wgu_pad, b_gate, norm_g, *, batch, seq, tc=256):
    T = z.shape[0]
    nc = seq // tc
    hp = GLA_HEADS_PER_STEP
    dk, dv = hp * GLA_DK, hp * GLA_DV
    kw = GLA_HEADS * GLA_DK
    k0 = kw // dk
    v0 = (2 * kw) // dv
    r0 = v0 + GLA_HEADS // hp
    row = lambda b, h, c: b * nc + c
    return pl.pallas_call(
        functools.partial(_gla_kernel, tc=tc),
        grid=(batch, GLA_HEADS // hp, nc),
        in_specs=[
            pl.BlockSpec((tc, dk), lambda b, h, c: (row(b, h, c), h)),
            pl.BlockSpec((tc, dk), lambda b, h, c: (row(b, h, c), k0 + h)),
            pl.BlockSpec((tc, dv), lambda b, h, c: (row(b, h, c), v0 + h)),
            pl.BlockSpec((tc, dv), lambda b, h, c: (row(b, h, c), r0 + h)),
            pl.BlockSpec((tc, LANES), lambda b, h, c: (row(b, h, c), 0)),
            pl.BlockSpec((LANES, dk), lambda b, h, c: (0, h)),
            pl.BlockSpec((1, dk), lambda b, h, c: (0, h)),
            pl.BlockSpec((1, dv), lambda b, h, c: (0, h)),
        ],
        out_specs=pl.BlockSpec((tc, dv), lambda b, h, c: (row(b, h, c), h)),
        out_shape=jax.ShapeDtypeStruct((T, GLA_HEADS * GLA_DV), BF16),
        scratch_shapes=[pltpu.VMEM((hp, GLA_DV, GLA_DK), F32)],
        compiler_params=_params(("parallel", "parallel", "arbitrary")),
        name="gla",
    )(z, z, z, z, g_low, wgu_pad, b_gate, norm_g)


def _lru_kernel(lx_ref, lg_ref, cw_ref, cb_ref, wa_ref, ba_ref, wi_ref, bi_ref, ap_ref, o_ref,
                tail_scr, h_scr, a_scr, u_scr, hs_scr, *, tc):
    @pl.when(pl.program_id(2) == 0)
    def _():
        tail_scr[...] = jnp.zeros_like(tail_scr)
        h_scr[...] = jnp.zeros_like(h_scr)

    x = lx_ref[...].astype(F32)
    tail_scr[SUBLANES:, :] = x
    xc = cb_ref[...]
    for i in range(LRU_CONV):
        s0 = SUBLANES - (LRU_CONV - 1) + i
        xc = xc + tail_scr[s0:s0 + tc, :] * cw_ref[i:i + 1, :]
    tail_scr[:SUBLANES, :] = x[tc - SUBLANES:, :]

    xb = xc.astype(BF16)
    ga = jax.nn.sigmoid(jnp.dot(xb, wa_ref[0], preferred_element_type=F32) + ba_ref[...])
    gi = jax.nn.sigmoid(jnp.dot(xb, wi_ref[0], preferred_element_type=F32) + bi_ref[...])
    log_a = -LRU_C * ga * jax.nn.softplus(-ap_ref[...])
    a = jnp.exp(log_a)
    u = xc * gi * jnp.sqrt(-jnp.tanh(log_a) * (a * a + 1.0))

    grouped = (tc // SUBLANES, SUBLANES, a.shape[1])
    a = a.reshape(grouped)
    u = u.reshape(grouped)
    sub = lax.broadcasted_iota(I32, grouped, 1)
    for s in (1, 2, 4):
        keep = sub >= s
        a_sh = jnp.where(keep, pltpu.roll(a, s, axis=1), 1.0)
        u_sh = jnp.where(keep, pltpu.roll(u, s, axis=1), 0.0)
        u = u + a * u_sh
        a = a * a_sh
    a_scr[...] = a.reshape(tc, grouped[2])
    u_scr[...] = u.reshape(tc, grouped[2])

    def body(g, hp):
        rows = pl.ds(pl.multiple_of(g * SUBLANES, SUBLANES), SUBLANES)
        hr = a_scr[rows, :] * hp + u_scr[rows, :]
        hs_scr[rows, :] = hr
        return jnp.broadcast_to(hr[SUBLANES - 1:, :], hr.shape)

    h_scr[...] = lax.fori_loop(0, tc // SUBLANES, body, h_scr[...], unroll=8)
    gate = jax.nn.gelu(lg_ref[...].astype(F32))
    o_ref[...] = (hs_scr[...] * gate).astype(o_ref.dtype)


def lru(z, conv_w, conv_b, w_a, b_a, w_i, b_i, a_param, *, batch, seq, lx_col, tc=512):
    T = z.shape[0]
    nc = seq // tc
    W = LRU_BLOCK
    lx0 = lx_col // W
    lg0 = lx0 + LRU_HEADS
    row = lambda b, h, c: b * nc + c
    vec = pl.BlockSpec((1, W), lambda b, h, c: (0, h))
    mat = pl.BlockSpec((1, W, W), lambda b, h, c: (h, 0, 0))
    return pl.pallas_call(
        functools.partial(_lru_kernel, tc=tc),
        grid=(batch, LRU_HEADS, nc),
        in_specs=[
            pl.BlockSpec((tc, W), lambda b, h, c: (row(b, h, c), lx0 + h)),
            pl.BlockSpec((tc, W), lambda b, h, c: (row(b, h, c), lg0 + h)),
            pl.BlockSpec((LRU_CONV, W), lambda b, h, c: (0, h)),
            vec, mat, vec, mat, vec, vec,
        ],
        out_specs=pl.BlockSpec((tc, W), lambda b, h, c: (row(b, h, c), h)),
        out_shape=jax.ShapeDtypeStruct((T, LRU_HEADS * W), BF16),
        scratch_shapes=[
            pltpu.VMEM((SUBLANES + tc, W), F32),
            pltpu.VMEM((SUBLANES, W), F32),
            pltpu.VMEM((tc, W), F32),
            pltpu.VMEM((tc, W), F32),
            pltpu.VMEM((tc, W), F32),
        ],
        compiler_params=_params(("parallel", "parallel", "arbitrary")),
        name="lru",
    )(z, z, conv_w, conv_b, w_a, b_a, w_i, b_i, a_param)


def _out_proj_kernel(go_ref, lo_ref, x_ref, wo_ref, gf_ref, wrh_ref, wrl_ref, br_ref,
                     x1_ref, hf_ref, idx_ref, gate_ref, cnt_ref):
    @pl.when(pl.program_id(0) == 0)
    def _():
        cnt_ref[...] = jnp.zeros_like(cnt_ref)

    half = go_ref.shape[1]
    mix = jnp.dot(go_ref[...], wo_ref[:half, :], preferred_element_type=F32)
    mix = mix + jnp.dot(lo_ref[...], wo_ref[half:, :], preferred_element_type=F32)
    x1 = x_ref[...] + mix
    x1_ref[...] = x1
    hf = _rms(x1, gf_ref[...])
    hf_ref[...] = hf
    hf_hi = hf.astype(BF16)
    hf_lo = (hf - hf_hi.astype(F32)).astype(BF16)

    logits = jnp.dot(hf_hi, wrh_ref[...], preferred_element_type=F32)
    logits = logits + jnp.dot(hf_lo, wrh_ref[...], preferred_element_type=F32)
    logits = logits + jnp.dot(hf_hi, wrl_ref[...], preferred_element_type=F32) + br_ref[...]
    lane = lax.broadcasted_iota(I32, logits.shape, 1)
    lanef = lane.astype(F32)
    neg = jnp.float32(-jnp.inf)
    work = jnp.where(lane < N_EXPERTS, logits, neg)
    vals, idxs = [], []
    hot = jnp.zeros(logits.shape, F32)
    for _ in range(TOP_K):
        m = jnp.max(work, axis=-1, keepdims=True)
        sel = jnp.min(jnp.where(work == m, lanef, float(LANES)), axis=-1, keepdims=True)
        picked = lanef == sel
        work = jnp.where(picked, neg, work)
        hot = jnp.where(picked, 1.0, hot)
        vals.append(m)
        idxs.append(sel)
    ex = [jnp.exp(v - vals[0]) for v in vals]
    den = ex[0] + ex[1] + ex[2] + ex[3]
    idx_out = jnp.zeros(logits.shape, F32)
    gate_out = jnp.zeros(logits.shape, F32)
    for k in range(TOP_K):
        idx_out = jnp.where(lane == k, idxs[k], idx_out)
        gate_out = jnp.where(lane == k, ex[k] / den, gate_out)
    idx_ref[...] = idx_out.astype(I32)
    gate_ref[...] = gate_out
    cnt_ref[...] += jnp.sum(hot, axis=0, keepdims=True)


def out_proj(gla_o, lru_o, x2, w_out, g_ffn, wr_hi, wr_lo, br_pad, *, tm=512):
    T, D = x2.shape
    half = gla_o.shape[1]
    row = lambda i: (i, 0)
    fixed = lambda i: (0, 0)
    return pl.pallas_call(
        _out_proj_kernel,
        grid=(T // tm,),
        in_specs=[
            pl.BlockSpec((tm, half), row),
            pl.BlockSpec((tm, half), row),
            pl.BlockSpec((tm, D), row),
            pl.BlockSpec((2 * half, D), fixed),
            pl.BlockSpec((1, D), fixed),
            pl.BlockSpec((D, LANES), fixed),
            pl.BlockSpec((D, LANES), fixed),
            pl.BlockSpec((1, LANES), fixed),
        ],
        out_specs=[
            pl.BlockSpec((tm, D), row),
            pl.BlockSpec((tm, D), row),
            pl.BlockSpec((tm, LANES), row),
            pl.BlockSpec((tm, LANES), row),
            pl.BlockSpec((SUBLANES, LANES), fixed),
        ],
        out_shape=[
            jax.ShapeDtypeStruct((T, D), F32),
            jax.ShapeDtypeStruct((T, D), F32),
            jax.ShapeDtypeStruct((T, LANES), I32),
            jax.ShapeDtypeStruct((T, LANES), F32),
            jax.ShapeDtypeStruct((SUBLANES, LANES), F32),
        ],
        compiler_params=_params(("arbitrary",)),
        name="out_proj",
    )(gla_o, lru_o, x2, w_out, g_ffn, wr_hi, wr_lo, br_pad)


def _rank_kernel(idx_ref, ps_ref, dest_ref, carry_scr):
    @pl.when(pl.program_id(0) == 0)
    def _():
        carry_scr[...] = jnp.zeros_like(carry_scr)

    idx = idx_ref[...]
    tm = idx.shape[0]
    lane = lax.broadcasted_iota(I32, idx.shape, 1)
    hot = [idx[:, k:k + 1] == lane for k in range(TOP_K)]
    oh = jnp.zeros(idx.shape, F32)
    for k in range(TOP_K):
        oh = oh + jnp.where(hot[k], 1.0, 0.0)
    lower = (lax.broadcasted_iota(I32, (tm, tm), 1) < lax.broadcasted_iota(I32, (tm, tm), 0))
    before = jnp.dot(jnp.where(lower, 1.0, 0.0).astype(BF16), oh.astype(BF16),
                     preferred_element_type=F32)
    base = before + (carry_scr[0:1, :] + ps_ref[...])
    out = jnp.zeros(idx.shape, F32)
    for k in range(TOP_K):
        dk = jnp.sum(jnp.where(hot[k], base, 0.0), axis=-1, keepdims=True)
        out = jnp.where(lane == k, dk, out)
    dest_ref[...] = out.astype(I32)
    carry_scr[...] += jnp.sum(oh, axis=0, keepdims=True)


def rank(idx_pad, ps_vec, *, tm=512):
    T = idx_pad.shape[0]
    return pl.pallas_call(
        _rank_kernel,
        grid=(T // tm,),
        in_specs=[
            pl.BlockSpec((tm, LANES), lambda i: (i, 0)),
            pl.BlockSpec((1, LANES), lambda i: (0, 0)),
        ],
        out_specs=pl.BlockSpec((tm, LANES), lambda i: (i, 0)),
        out_shape=jax.ShapeDtypeStruct((T, LANES), I32),
        scratch_shapes=[pltpu.VMEM((SUBLANES, LANES), F32)],
        compiler_params=_params(("arbitrary",)),
        name="rank",
    )(idx_pad, ps_vec)


def _scatter_kernel(zrow_ref, dest_ref, hp_ref, xs_ref, zero_scr, sem, zsem, *, tm):
    @pl.when(pl.program_id(0) == 0)
    def _():
        zero_scr[...] = jnp.zeros_like(zero_scr)

        def zero_copy(e):
            r0 = pl.multiple_of(zrow_ref[e], SUBLANES)
            return pltpu.make_async_copy(zero_scr, xs_ref.at[pl.ds(r0, EXPERT_TILE)], zsem)

        def zstart(e, carry):
            @pl.when(zrow_ref[e] >= 0)
            def _():
                zero_copy(e).start()
            return carry

        def zwait(e, carry):
            @pl.when(zrow_ref[e] >= 0)
            def _():
                zero_copy(e).wait()
            return carry

        lax.fori_loop(0, zrow_ref.shape[0], zstart, 0)
        lax.fori_loop(0, zrow_ref.shape[0], zwait, 0)

    def issue(t, carry):
        for k in range(TOP_K):
            d = dest_ref[t * TOP_K + k]
            pltpu.make_async_copy(hp_ref.at[pl.ds(t, 1)], xs_ref.at[pl.ds(d, 1)], sem).start(priority=k % 2)
        return carry

    for t in range(tm):
        issue(t, 0)
    for k in range(TOP_K):
        pltpu.make_async_copy(hp_ref, xs_ref.at[pl.ds(0, tm)], sem).wait()


def scatter_rows(zrow, dest_flat, hp, n_rows, *, tm=512):
    T, W = hp.shape
    return pl.pallas_call(
        functools.partial(_scatter_kernel, tm=tm),
        grid_spec=pltpu.PrefetchScalarGridSpec(
            num_scalar_prefetch=1,
            grid=(T // tm,),
            in_specs=[
                pl.BlockSpec((tm * TOP_K,), lambda i, zr: (i,), memory_space=pltpu.SMEM),
                pl.BlockSpec((tm, W), lambda i, zr: (i, 0)),
            ],
            out_specs=pl.BlockSpec(memory_space=pl.ANY),
            scratch_shapes=[
                pltpu.VMEM((EXPERT_TILE, W), hp.dtype),
                pltpu.SemaphoreType.DMA(()),
                pltpu.SemaphoreType.DMA(()),
            ],
        ),
        out_shape=jax.ShapeDtypeStruct((n_rows, W), hp.dtype),
        compiler_params=_params(("arbitrary",)),
        name="scatter_rows",
    )(zrow, dest_flat, hp)


def _tile_changed(te_ref, i):
    return (i == 0) | (te_ref[i] != te_ref[jnp.maximum(i - 1, 0)])


def _stream_expert_weights(te_ref, tr_ref, ne_ref, fetch, cast, nj):
    j = pl.program_id(0)
    i = pl.program_id(1)

    @pl.when((tr_ref[i] > 0) & _tile_changed(te_ref, i))
    def _():
        @pl.when((j == 0) & (i == 0))
        def _():
            for c in fetch(te_ref[0], 0):
                c.start()

        for c in fetch(te_ref[i], j):
            c.wait()
        cast()
        wrap = ne_ref[i] < 0
        e_next = jnp.where(wrap, te_ref[0], ne_ref[i])
        j_next = jnp.where(wrap, j + 1, j)

        @pl.when(j_next < nj)
        def _():
            for c in fetch(e_next, j_next):
                c.start()


def _for_row_count(rows, out_ref, compute):
    for m in range(ROW_STEP, EXPERT_TILE + 1, ROW_STEP):
        @pl.when((rows > m - ROW_STEP) & (rows <= m))
        def _(m=m):
            compute(m)

    @pl.when(rows == 0)
    def _():
        out_ref[...] = jnp.zeros_like(out_ref)


def _swiglu_rows(xs_ref, wg_scr, wu_scr, bg_ref, bu_ref, h_ref, m):
    xb = xs_ref[:m, :].astype(BF16)
    tn = h_ref.shape[1]
    for n0 in range(0, tn, MXU_N):
        cols = slice(n0, n0 + MXU_N)

        def proj(w_scr, b_ref):
            return jnp.dot(xb, w_scr[:, cols], preferred_element_type=F32) + b_ref[0, :, cols]

        gate = jnp.minimum(proj(wg_scr, bg_ref), SWIGLU_LIMIT)
        up = jnp.clip(proj(wu_scr, bu_ref), -SWIGLU_LIMIT, SWIGLU_LIMIT)
        glu = gate * jax.nn.sigmoid(SWIGLU_ALPHA * gate)
        h_ref[:m, cols] = ((up + 1.0) * glu).astype(h_ref.dtype)
    if m < h_ref.shape[0]:
        h_ref[m:, :] = jnp.zeros((h_ref.shape[0] - m, tn), h_ref.dtype)


def _expert_up_kernel(te_ref, tr_ref, ne_ref, nu_ref, xs_ref, w_hbm, bg_ref, bu_ref, h_ref,
                      stage, wg_scr, wu_scr, sems, *, nj):
    tn = h_ref.shape[1]
    F = nj * tn

    def fetch(e, j):
        c0 = pl.multiple_of(j * tn, tn)
        c1 = pl.multiple_of(F + j * tn, tn)
        return (pltpu.make_async_copy(w_hbm.at[e, :, pl.ds(c0, tn)], stage.at[0], sems.at[0]),
                pltpu.make_async_copy(w_hbm.at[e, :, pl.ds(c1, tn)], stage.at[1], sems.at[1]))

    def cast():
        def chunk(c, carry):
            r = pl.ds(pl.multiple_of(c * CAST_ROWS, CAST_ROWS), CAST_ROWS)
            wg_scr[r, :] = stage[0, r, :].astype(BF16)
            wu_scr[r, :] = stage[1, r, :].astype(BF16)
            return carry

        lax.fori_loop(0, wg_scr.shape[0] // CAST_ROWS, chunk, 0)

    _stream_expert_weights(te_ref, tr_ref, ne_ref, fetch, cast, nj)
    rows = tr_ref[pl.program_id(1)]
    _for_row_count(rows, h_ref, functools.partial(_swiglu_rows, xs_ref, wg_scr, wu_scr, bg_ref, bu_ref, h_ref))


def expert_up(tile_e, tile_rows, next_e, n_used, xs, w_gate_up, b_gate_up3, *, tm=EXPERT_TILE, tn=1024):
    P, D = xs.shape
    F = w_gate_up.shape[2] // 2
    nj = F // tn
    row = lambda i, nu: jnp.minimum(i, nu[0] - 1)
    return pl.pallas_call(
        functools.partial(_expert_up_kernel, nj=nj),
        grid_spec=pltpu.PrefetchScalarGridSpec(
            num_scalar_prefetch=4,
            grid=(nj, P // tm),
            in_specs=[
                pl.BlockSpec((tm, D), lambda j, i, te, tr, ne, nu: (row(i, nu), 0)),
                pl.BlockSpec(memory_space=pl.ANY),
                pl.BlockSpec((1, 1, tn), lambda j, i, te, tr, ne, nu: (te[i], 0, j)),
                pl.BlockSpec((1, 1, tn), lambda j, i, te, tr, ne, nu: (te[i], 0, nj + j)),
            ],
            out_specs=pl.BlockSpec((tm, tn), lambda j, i, te, tr, ne, nu: (i, j)),
            scratch_shapes=[
                pltpu.VMEM((2, D, tn), F32),
                pltpu.VMEM((D, tn), BF16),
                pltpu.VMEM((D, tn), BF16),
                pltpu.SemaphoreType.DMA((2,)),
            ],
        ),
        out_shape=jax.ShapeDtypeStruct((P, F), BF16),
        compiler_params=_params(("arbitrary", "arbitrary")),
        name="expert_up",
    )(tile_e, tile_rows, next_e, n_used, xs, w_gate_up, b_gate_up3, b_gate_up3)


def _down_rows(h_ref, wd_scr, bd_ref, y_ref, m):
    y_ref[:m, :] = jnp.dot(h_ref[:m, :], wd_scr[...], preferred_element_type=F32) + bd_ref[0]
    if m < y_ref.shape[0]:
        y_ref[m:, :] = jnp.zeros((y_ref.shape[0] - m, y_ref.shape[1]), y_ref.dtype)


def _expert_dn_kernel(te_ref, tr_ref, ne_ref, nu_ref, h_ref, w_hbm, bd_ref, y_ref, stage, wd_scr, sem, *, nj):
    tn = y_ref.shape[1]

    def fetch(e, j):
        c0 = pl.multiple_of(j * tn, tn)
        return (pltpu.make_async_copy(w_hbm.at[e, :, pl.ds(c0, tn)], stage, sem),)

    def cast():
        def chunk(c, carry):
            r = pl.ds(pl.multiple_of(c * CAST_ROWS, CAST_ROWS), CAST_ROWS)
            wd_scr[r, :] = stage[r, :].astype(BF16)
            return carry

        lax.fori_loop(0, wd_scr.shape[0] // CAST_ROWS, chunk, 0)

    _stream_expert_weights(te_ref, tr_ref, ne_ref, fetch, cast, nj)
    rows = tr_ref[pl.program_id(1)]
    _for_row_count(rows, y_ref, functools.partial(_down_rows, h_ref, wd_scr, bd_ref, y_ref))


def expert_dn(tile_e, tile_rows, next_e, n_used, h, w_down, b_down3, *, tm=EXPERT_TILE, tn=2048):
    P, F = h.shape
    D = w_down.shape[2]
    nj = D // tn
    row = lambda i, nu: jnp.minimum(i, nu[0] - 1)
    return pl.pallas_call(
        functools.partial(_expert_dn_kernel, nj=nj),
        grid_spec=pltpu.PrefetchScalarGridSpec(
            num_scalar_prefetch=4,
            grid=(nj, P // tm),
            in_specs=[
                pl.BlockSpec((tm, F), lambda j, i, te, tr, ne, nu: (row(i, nu), 0)),
                pl.BlockSpec(memory_space=pl.ANY),
                pl.BlockSpec((1, 1, tn), lambda j, i, te, tr, ne, nu: (te[i], 0, j)),
            ],
            out_specs=pl.BlockSpec((tm, tn), lambda j, i, te, tr, ne, nu: (i, j)),
            scratch_shapes=[
                pltpu.VMEM((F, tn), F32),
                pltpu.VMEM((F, tn), BF16),
                pltpu.SemaphoreType.DMA(()),
            ],
        ),
        out_shape=jax.ShapeDtypeStruct((P, D), F32),
        compiler_params=_params(("arbitrary", "arbitrary")),
        name="expert_dn",
    )(tile_e, tile_rows, next_e, n_used, h, w_down, b_down3)


def _combine_kernel(dcur_ref, dnxt_ref, x1_ref, gate_ref, gfin_ref, y_ref, o_ref, buf, sems, *, tm):
    i = pl.program_id(0)
    n = pl.num_programs(0)

    def gather(dest_ref, h, unrolled):
        def issue(t, carry):
            for k in range(TOP_K):
                d = dest_ref[(h * tm + t) * TOP_K + k]
                pltpu.make_async_copy(y_ref.at[pl.ds(d, 1)], buf.at[h, k, pl.ds(t, 1)],
                                      sems.at[h]).start(priority=k % 2)
            return carry

        if unrolled:
            for t in range(tm):
                issue(t, 0)
        else:
            lax.fori_loop(0, tm, issue, 0, unroll=8)

    @pl.when(i == 0)
    def _():
        gather(dcur_ref, 0, False)
        gather(dcur_ref, 1, False)

    for h in range(2):
        rows = slice(h * tm, (h + 1) * tm)
        for k in range(TOP_K):
            pltpu.make_async_copy(y_ref.at[pl.ds(0, tm)], buf.at[h, k], sems.at[h]).wait()
        gates = gate_ref[rows, :]
        moe = gates[:, 0:1] * buf[h, 0]
        for k in range(1, TOP_K):
            moe = moe + gates[:, k:k + 1] * buf[h, k]
        o_ref[rows, :] = _rms(x1_ref[rows, :] + moe, gfin_ref[...])

        @pl.when(i + 1 < n)
        def _(h=h):
            gather(dnxt_ref, h, True)


def combine(dest_flat, x1, gates, g_fin, y, *, tm=256):
    T, D = x1.shape
    step = 2 * tm
    n = T // step
    blk = step * TOP_K
    return pl.pallas_call(
        functools.partial(_combine_kernel, tm=tm),
        grid=(n,),
        in_specs=[
            pl.BlockSpec((blk,), lambda i: (i,), memory_space=pltpu.SMEM),
            pl.BlockSpec((blk,), lambda i: (jnp.minimum(i + 1, n - 1),), memory_space=pltpu.SMEM),
            pl.BlockSpec((step, D), lambda i: (i, 0)),
            pl.BlockSpec((step, LANES), lambda i: (i, 0)),
            pl.BlockSpec((1, D), lambda i: (0, 0)),
            pl.BlockSpec(memory_space=pl.ANY),
        ],
        out_specs=pl.BlockSpec((step, D), lambda i: (i, 0)),
        out_shape=jax.ShapeDtypeStruct((T, D), F32),
        scratch_shapes=[pltpu.VMEM((2, TOP_K, tm, D), F32), pltpu.SemaphoreType.DMA((2,))],
        compiler_params=_params(("arbitrary",)),
        name="combine",
    )(dest_flat, dest_flat, x1, gates, g_fin, y)


def _tile_plan(counts, n_tiles, tile):
    tiles_e = (counts + tile - 1) // tile
    ends = jnp.cumsum(tiles_e)
    starts = ends - tiles_e
    pstart = starts * tile
    n_used = ends[-1]
    t = jnp.arange(n_tiles, dtype=I32)
    tc = jnp.minimum(t, n_used - 1)
    expert_of = lambda tiles: jnp.sum((tiles[:, None] >= ends[None, :]).astype(I32), axis=1)
    tile_e = expert_of(tc)
    mine = tile_e[:, None] == jnp.arange(counts.shape[0], dtype=I32)[None, :]
    pick = lambda table: jnp.sum(jnp.where(mine, table[None, :], 0), axis=1)
    left = pick(counts) - (tc - pick(starts)) * tile
    tile_rows = jnp.where(t < n_used, jnp.clip(left, 0, tile), 0)
    nxt = pick(ends)
    next_e = jnp.where(nxt < n_used, expert_of(nxt), -1)
    spare = n_used + jnp.arange(counts.shape[0], dtype=I32)
    zrow = jnp.concatenate([jnp.where(tiles_e > 0, (ends - 1) * tile, -1),
                            jnp.where(spare < n_tiles, spare * tile, -1)])
    return (pstart.astype(I32), tile_e.astype(I32), tile_rows.astype(I32), next_e.astype(I32),
            n_used.reshape(1).astype(I32), zrow.astype(I32))


def kernel(x, norm_mix_g, w_in, gla_w_gate_up, gla_b_gate, gla_norm_g, lru_conv_w, lru_conv_b, lru_w_a,
           lru_b_a, lru_w_i, lru_b_i, lru_a_param, w_out, norm_ffn_g, w_router, b_router, w_gate_up,
           b_gate_up, w_down, b_down, norm_final_g):
    B, S, D = x.shape
    assert w_in.shape[0] == 1, "the final norm is fused into the single layer's combine stage"
    T = B * S
    x2 = x.reshape(T, D)
    kw = GLA_HEADS * GLA_DK
    gw = GLA_HEADS * GLA_DV
    lw = LRU_HEADS * LRU_BLOCK
    g0 = 2 * kw + 2 * gw

    w_in0 = w_in[0]
    w_main = jnp.concatenate([w_in0[:, :g0], w_in0[:, g0 + GLA_GATE_RANK:]], axis=1).astype(BF16)
    w_glow = jnp.pad(w_in0[:, g0:g0 + GLA_GATE_RANK], ((0, 0), (0, LANES - GLA_GATE_RANK))).astype(BF16)
    z, g_low = in_proj(x2, norm_mix_g[0].reshape(1, D), w_main, w_glow)

    wgu_pad = jnp.pad(gla_w_gate_up[0], ((0, LANES - GLA_GATE_RANK), (0, 0))).astype(BF16)
    gla_o = gla(z, g_low, wgu_pad, gla_b_gate[0].reshape(1, kw), gla_norm_g[0].reshape(1, gw),
                batch=B, seq=S)
    lru_o = lru(z, lru_conv_w[0], lru_conv_b[0].reshape(1, lw), lru_w_a[0].astype(BF16),
                lru_b_a[0].reshape(1, lw), lru_w_i[0].astype(BF16), lru_b_i[0].reshape(1, lw),
                lru_a_param[0].reshape(1, lw), batch=B, seq=S, lx_col=g0)

    wr_pad = jnp.pad(w_router[0], ((0, 0), (0, LANES - N_EXPERTS)))
    wr_hi = wr_pad.astype(BF16)
    wr_lo = (wr_pad - wr_hi.astype(F32)).astype(BF16)
    br_pad = jnp.pad(b_router[0], (0, LANES - N_EXPERTS)).reshape(1, LANES)
    x1, hf, idx_pad, gates, cnt = out_proj(gla_o, lru_o, x2, w_out[0].astype(BF16),
                                           norm_ffn_g[0].reshape(1, D), wr_hi, wr_lo, br_pad)

    n_tiles = (T * TOP_K) // EXPERT_TILE + N_EXPERTS
    pstart, tile_e, tile_rows, next_e, n_used, zrow = _tile_plan(
        cnt[0, :N_EXPERTS].astype(I32), n_tiles, EXPERT_TILE)
    ps_vec = jnp.pad(pstart.astype(F32), (0, LANES - N_EXPERTS)).reshape(1, LANES)
    dest_pad = rank(idx_pad, ps_vec)
    dest_flat = dest_pad[:, :TOP_K].reshape(-1)

    xs = scatter_rows(zrow, dest_flat, hf, n_tiles * EXPERT_TILE)
    E, _, F2 = w_gate_up.shape[1:]
    h = expert_up(tile_e, tile_rows, next_e, n_used, xs, w_gate_up[0], b_gate_up[0].reshape(E, 1, F2))
    y = expert_dn(tile_e, tile_rows, next_e, n_used, h, w_down[0], b_down[0].reshape(E, 1, D))
    out = combine(dest_flat, x1, gates, norm_final_g.reshape(1, D), y)
    return out.reshape(B, S, D)
```

```python
import functools

import jax
import jax.numpy as jnp
from jax import lax
from jax.experimental import pallas as pl
from jax.experimental.pallas import tpu as pltpu

F32 = jnp.float32
BF16 = jnp.bfloat16
I32 = jnp.int32

EPS = 1e-6
LANES = 128
SUBLANES = 8
MXU_N = 256
VMEM_LIMIT = 56 * 1024 * 1024

GLA_HEADS = 4
GLA_DK = 128
GLA_DV = 256
GLA_GATE_RANK = 16
GLA_GATE_NORM = 16.0
GLA_CHUNK = 64
GLA_HEADS_PER_STEP = 4
LRU_HEADS = 4
LRU_BLOCK = 256
LRU_CONV = 4
LRU_C = 8.0
N_EXPERTS = 32
TOP_K = 4
SWIGLU_LIMIT = 7.0
SWIGLU_ALPHA = 1.702

EXPERT_TILE = 512
ROW_STEP = 128
CAST_ROWS = 32


def _params(sem, vmem=VMEM_LIMIT):
    return pltpu.CompilerParams(dimension_semantics=sem, vmem_limit_bytes=vmem)


def _rms(x, g):
    ms = jnp.mean(x * x, axis=-1, keepdims=True)
    return x * lax.rsqrt(ms + EPS) * g


def _split3(x):
    a1 = x.astype(BF16)
    r1 = x - a1.astype(F32)
    a2 = r1.astype(BF16)
    a3 = (r1 - a2.astype(F32)).astype(BF16)
    return a1, a2, a3


def _in_proj_kernel(x_ref, g_ref, w_ref, wgl_ref, z_ref, gl_ref, h_scr):
    @pl.when(pl.program_id(1) == 0)
    def _():
        h = _rms(x_ref[...], g_ref[...]).astype(BF16)
        h_scr[...] = h
        gl_ref[...] = jnp.dot(h, wgl_ref[...], preferred_element_type=F32)

    z_ref[...] = jnp.dot(h_scr[...], w_ref[...], preferred_element_type=F32).astype(z_ref.dtype)


def in_proj(x2, g, w_main, w_glow, *, tm=1024, tn=1024):
    T, D = x2.shape
    N = w_main.shape[1]
    return pl.pallas_call(
        _in_proj_kernel,
        grid=(T // tm, N // tn),
        in_specs=[
            pl.BlockSpec((tm, D), lambda i, j: (i, 0)),
            pl.BlockSpec((1, D), lambda i, j: (0, 0)),
            pl.BlockSpec((D, tn), lambda i, j: (0, j)),
            pl.BlockSpec((D, LANES), lambda i, j: (0, 0)),
        ],
        out_specs=[
            pl.BlockSpec((tm, tn), lambda i, j: (i, j)),
            pl.BlockSpec((tm, LANES), lambda i, j: (i, 0)),
        ],
        out_shape=[jax.ShapeDtypeStruct((T, N), BF16), jax.ShapeDtypeStruct((T, LANES), F32)],
        scratch_shapes=[pltpu.VMEM((tm, D), BF16)],
        compiler_params=_params(("parallel", "arbitrary")),
        name="in_proj",
    )(x2, g, w_main, w_glow)


def _gla_kernel(q_ref, k_ref, v_ref, r_ref, gl_ref, wgu_ref, bg_ref, ng_ref, o_ref, st_scr, *, tc):
    @pl.when(pl.program_id(2) == 0)
    def _():
        st_scr[...] = jnp.zeros_like(st_scr)

    ri = lax.broadcasted_iota(I32, (tc, tc), 0)
    ci = lax.broadcasted_iota(I32, (tc, tc), 1)
    same = (ri // GLA_CHUNK) == (ci // GLA_CHUNK)
    cum_m = jnp.where(same & (ci <= ri), 1.0, 0.0).astype(BF16)
    tri = (lax.broadcasted_iota(I32, (GLA_CHUNK, GLA_CHUNK), 1)
           <= lax.broadcasted_iota(I32, (GLA_CHUNK, GLA_CHUNK), 0))
    scale = GLA_DK ** -0.5
    nt = (((1,), (1,)), ((), ()))
    tn = (((0,), (0,)), ((), ()))
    chunks = range(tc // GLA_CHUNK)
    rows = [slice(c * GLA_CHUNK, (c + 1) * GLA_CHUNK) for c in chunks]
    pre = jnp.dot(gl_ref[...].astype(BF16), wgu_ref[...], preferred_element_type=F32) + bg_ref[...]
    log_g = jax.nn.log_sigmoid(pre) * (1.0 / GLA_GATE_NORM)
    bcum_all = None
    for piece in _split3(log_g):
        part = jnp.dot(cum_m, piece, preferred_element_type=F32)
        bcum_all = part if bcum_all is None else bcum_all + part

    for hd in range(GLA_HEADS_PER_STEP):
        kc = slice(hd * GLA_DK, (hd + 1) * GLA_DK)
        vc = slice(hd * GLA_DV, (hd + 1) * GLA_DV)
        qe, ke, kd, dec = [], [], [], []
        for sl in rows:
            b = bcum_all[sl, kc]
            bl = b[GLA_CHUNK - 1:GLA_CHUNK]
            qf = q_ref[sl, kc].astype(F32) * scale
            kf = k_ref[sl, kc].astype(F32)
            qe.append((qf * jnp.exp(b)).astype(BF16))
            ke.append((kf * jnp.exp(-b)).astype(BF16))
            kd.append((kf * jnp.exp(bl - b)).astype(BF16))
            dec.append(jnp.exp(bl))
        scores = [jnp.where(tri, lax.dot_general(qe[c], ke[c], nt, preferred_element_type=F32), 0.0).astype(BF16)
                  for c in chunks]
        st = st_scr[hd]
        outs = []
        for c in chunks:
            vb = v_ref[rows[c], vc]
            o = jnp.dot(scores[c], vb, preferred_element_type=F32)
            o = o + lax.dot_general(qe[c], st.astype(BF16), nt, preferred_element_type=F32)
            st = st * dec[c] + lax.dot_general(vb, kd[c], tn, preferred_element_type=F32)
            outs.append(o)
        st_scr[hd] = st
        o = jnp.concatenate(outs, axis=0)
        on = _rms(o, ng_ref[:, vc])
        r = r_ref[:, vc].astype(F32)
        o_ref[:, vc] = (jax.nn.silu(r) * on).astype(o_ref.dtype)


def gla(z, g_low, wgu_pad, b_gate, norm_g, *, batch, seq, tc=256):
    T = z.shape[0]
    nc = seq // tc
    hp = GLA_HEADS_PER_STEP
    dk, dv = hp * GLA_DK, hp * GLA_DV
    kw = GLA_HEADS * GLA_DK
    k0 = kw // dk
    v0 = (2 * kw) // dv
    r0 = v0 + GLA_HEADS // hp
    row = lambda b, h, c: b * nc + c
    return pl.pallas_call(
        functools.partial(_gla_kernel, tc=tc),
        grid=(batch, GLA_HEADS // hp, nc),
        in_specs=[
            pl.BlockSpec((tc, dk), lambda b, h, c: (row(b, h, c), h)),
            pl.BlockSpec((tc, dk), lambda b, h, c: (row(b, h, c), k0 + h)),
            pl.BlockSpec((tc, dv), lambda b, h, c: (row(b, h, c), v0 + h)),
            pl.BlockSpec((tc, dv), lambda b, h, c: (row(b, h, c), r0 + h)),
            pl.BlockSpec((tc, LANES), lambda b, h, c: (row(b, h, c), 0)),
            pl.BlockSpec((LANES, dk), lambda b, h, c: (0, h)),
            pl.BlockSpec((1, dk), lambda b, h, c: (0, h)),
            pl.BlockSpec((1, dv), lambda b, h, c: (0, h)),
        ],
        out_specs=pl.BlockSpec((tc, dv), lambda b, h, c: (row(b, h, c), h)),
        out_shape=jax.ShapeDtypeStruct((T, GLA_HEADS * GLA_DV), BF16),
        scratch_shapes=[pltpu.VMEM((hp, GLA_DV, GLA_DK), F32)],
        compiler_params=_params(("parallel", "parallel", "arbitrary")),
        name="gla",
    )(z, z, z, z, g_low, wgu_pad, b_gate, norm_g)


def _lru_kernel(lx_ref, lg_ref, cw_ref, cb_ref, wa_ref, ba_ref, wi_ref, bi_ref, ap_ref, o_ref,
                tail_scr, h_scr, a_scr, u_scr, hs_scr, *, tc):
    @pl.when(pl.program_id(2) == 0)
    def _():
        tail_scr[...] = jnp.zeros_like(tail_scr)
        h_scr[...] = jnp.zeros_like(h_scr)

    x = lx_ref[...].astype(F32)
    tail_scr[SUBLANES:, :] = x
    xc = cb_ref[...]
    for i in range(LRU_CONV):
        s0 = SUBLANES - (LRU_CONV - 1) + i
        xc = xc + tail_scr[s0:s0 + tc, :] * cw_ref[i:i + 1, :]
    tail_scr[:SUBLANES, :] = x[tc - SUBLANES:, :]

    xb = xc.astype(BF16)
    ga = jax.nn.sigmoid(jnp.dot(xb, wa_ref[0], preferred_element_type=F32) + ba_ref[...])
    gi = jax.nn.sigmoid(jnp.dot(xb, wi_ref[0], preferred_element_type=F32) + bi_ref[...])
    log_a = -LRU_C * ga * jax.nn.softplus(-ap_ref[...])
    a = jnp.exp(log_a)
    u = xc * gi * jnp.sqrt(-jnp.tanh(log_a) * (a * a + 1.0))

    grouped = (tc // SUBLANES, SUBLANES, a.shape[1])
    a = a.reshape(grouped)
    u = u.reshape(grouped)
    sub = lax.broadcasted_iota(I32, grouped, 1)
    for s in (1, 2, 4):
        keep = sub >= s
        a_sh = jnp.where(keep, pltpu.roll(a, s, axis=1), 1.0)
        u_sh = jnp.where(keep, pltpu.roll(u, s, axis=1), 0.0)
        u = u + a * u_sh
        a = a * a_sh
    a_scr[...] = a.reshape(tc, grouped[2])
    u_scr[...] = u.reshape(tc, grouped[2])

    def body(g, hp):
        rows = pl.ds(pl.multiple_of(g * SUBLANES, SUBLANES), SUBLANES)
        hr = a_scr[rows, :] * hp + u_scr[rows, :]
        hs_scr[rows, :] = hr
        return jnp.broadcast_to(hr[SUBLANES - 1:, :], hr.shape)

    h_scr[...] = lax.fori_loop(0, tc // SUBLANES, body, h_scr[...], unroll=8)
    gate = jax.nn.gelu(lg_ref[...].astype(F32))
    o_ref[...] = (hs_scr[...] * gate).astype(o_ref.dtype)


def lru(z, conv_w, conv_b, w_a, b_a, w_i, b_i, a_param, *, batch, seq, lx_col, tc=512):
    T = z.shape[0]
    nc = seq // tc
    W = LRU_BLOCK
    lx0 = lx_col // W
    lg0 = lx0 + LRU_HEADS
    row = lambda b, h, c: b * nc + c
    vec = pl.BlockSpec((1, W), lambda b, h, c: (0, h))
    mat = pl.BlockSpec((1, W, W), lambda b, h, c: (h, 0, 0))
    return pl.pallas_call(
        functools.partial(_lru_kernel, tc=tc),
        grid=(batch, LRU_HEADS, nc),
        in_specs=[
            pl.BlockSpec((tc, W), lambda b, h, c: (row(b, h, c), lx0 + h)),
            pl.BlockSpec((tc, W), lambda b, h, c: (row(b, h, c), lg0 + h)),
            pl.BlockSpec((LRU_CONV, W), lambda b, h, c: (0, h)),
            vec, mat, vec, mat, vec, vec,
        ],
        out_specs=pl.BlockSpec((tc, W), lambda b, h, c: (row(b, h, c), h)),
        out_shape=jax.ShapeDtypeStruct((T, LRU_HEADS * W), BF16),
        scratch_shapes=[
            pltpu.VMEM((SUBLANES + tc, W), F32),
            pltpu.VMEM((SUBLANES, W), F32),
            pltpu.VMEM((tc, W), F32),
            pltpu.VMEM((tc, W), F32),
            pltpu.VMEM((tc, W), F32),
        ],
        compiler_params=_params(("parallel", "parallel", "arbitrary")),
        name="lru",
    )(z, z, conv_w, conv_b, w_a, b_a, w_i, b_i, a_param)


def _out_proj_kernel(go_ref, lo_ref, x_ref, wo_ref, gf_ref, wrh_ref, wrl_ref, br_ref,
                     x1_ref, hf_ref, idx_ref, gate_ref, cnt_ref):
    @pl.when(pl.program_id(0) == 0)
    def _():
        cnt_ref[...] = jnp.zeros_like(cnt_ref)

    half = go_ref.shape[1]
    mix = jnp.dot(go_ref[...], wo_ref[:half, :], preferred_element_type=F32)
    mix = mix + jnp.dot(lo_ref[...], wo_ref[half:, :], preferred_element_type=F32)
    x1 = x_ref[...] + mix
    x1_ref[...] = x1
    hf = _rms(x1, gf_ref[...])
    hf_ref[...] = hf
    hf_hi = hf.astype(BF16)
    hf_lo = (hf - hf_hi.astype(F32)).astype(BF16)

    logits = jnp.dot(hf_hi, wrh_ref[...], preferred_element_type=F32)
    logits = logits + jnp.dot(hf_lo, wrh_ref[...], preferred_element_type=F32)
    logits = logits + jnp.dot(hf_hi, wrl_ref[...], preferred_element_type=F32) + br_ref[...]
    lane = lax.broadcasted_iota(I32, logits.shape, 1)
    lanef = lane.astype(F32)
    neg = jnp.float32(-jnp.inf)
    work = jnp.where(lane < N_EXPERTS, logits, neg)
    vals, idxs = [], []
    hot = jnp.zeros(logits.shape, F32)
    for _ in range(TOP_K):
        m = jnp.max(work, axis=-1, keepdims=True)
        sel = jnp.min(jnp.where(work == m, lanef, float(LANES)), axis=-1, keepdims=True)
        picked = lanef == sel
        work = jnp.where(picked, neg, work)
        hot = jnp.where(picked, 1.0, hot)
        vals.append(m)
        idxs.append(sel)
    ex = [jnp.exp(v - vals[0]) for v in vals]
    den = ex[0] + ex[1] + ex[2] + ex[3]
    idx_out = jnp.zeros(logits.shape, F32)
    gate_out = jnp.zeros(logits.shape, F32)
    for k in range(TOP_K):
        idx_out = jnp.where(lane == k, idxs[k], idx_out)
        gate_out = jnp.where(lane == k, ex[k] / den, gate_out)
    idx_ref[...] = idx_out.astype(I32)
    gate_ref[...] = gate_out
    cnt_ref[...] += jnp.sum(hot, axis=0, keepdims=True)


def out_proj(gla_o, lru_o, x2, w_out, g_ffn, wr_hi, wr_lo, br_pad, *, tm=512):
    T, D = x2.shape
    half = gla_o.shape[1]
    row = lambda i: (i, 0)
    fixed = lambda i: (0, 0)
    return pl.pallas_call(
        _out_proj_kernel,
        grid=(T // tm,),
        in_specs=[
            pl.BlockSpec((tm, half), row),
            pl.BlockSpec((tm, half), row),
            pl.BlockSpec((tm, D), row),
            pl.BlockSpec((2 * half, D), fixed),
            pl.BlockSpec((1, D), fixed),
            pl.BlockSpec((D, LANES), fixed),
            pl.BlockSpec((D, LANES), fixed),
            pl.BlockSpec((1, LANES), fixed),
        ],
        out_specs=[
            pl.BlockSpec((tm, D), row),
            pl.BlockSpec((tm, D), row),
            pl.BlockSpec((tm, LANES), row),
            pl.BlockSpec((tm, LANES), row),
            pl.BlockSpec((SUBLANES, LANES), fixed),
        ],
        out_shape=[
            jax.ShapeDtypeStruct((T, D), F32),
            jax.ShapeDtypeStruct((T, D), F32),
            jax.ShapeDtypeStruct((T, LANES), I32),
            jax.ShapeDtypeStruct((T, LANES), F32),
            jax.ShapeDtypeStruct((SUBLANES, LANES), F32),
        ],
        compiler_params=_params(("arbitrary",)),
        name="out_proj",
    )(gla_o, lru_o, x2, w_out, g_ffn, wr_hi, wr_lo, br_pad)


def _rank_kernel(idx_ref, ps_ref, dest_ref, carry_scr):
    @pl.when(pl.program_id(0) == 0)
    def _():
        carry_scr[...] = jnp.zeros_like(carry_scr)

    idx = idx_ref[...]
    tm = idx.shape[0]
    lane = lax.broadcasted_iota(I32, idx.shape, 1)
    hot = [idx[:, k:k + 1] == lane for k in range(TOP_K)]
    oh = jnp.zeros(idx.shape, F32)
    for k in range(TOP_K):
        oh = oh + jnp.where(hot[k], 1.0, 0.0)
    lower = (lax.broadcasted_iota(I32, (tm, tm), 1) < lax.broadcasted_iota(I32, (tm, tm), 0))
    before = jnp.dot(jnp.where(lower, 1.0, 0.0).astype(BF16), oh.astype(BF16),
                     preferred_element_type=F32)
    base = before + (carry_scr[0:1, :] + ps_ref[...])
    out = jnp.zeros(idx.shape, F32)
    for k in range(TOP_K):
        dk = jnp.sum(jnp.where(hot[k], base, 0.0), axis=-1, keepdims=True)
        out = jnp.where(lane == k, dk, out)
    dest_ref[...] = out.astype(I32)
    carry_scr[...] += jnp.sum(oh, axis=0, keepdims=True)


def rank(idx_pad, ps_vec, *, tm=512):
    T = idx_pad.shape[0]
    return pl.pallas_call(
        _rank_kernel,
        grid=(T // tm,),
        in_specs=[
            pl.BlockSpec((tm, LANES), lambda i: (i, 0)),
            pl.BlockSpec((1, LANES), lambda i: (0, 0)),
        ],
        out_specs=pl.BlockSpec((tm, LANES), lambda i: (i, 0)),
        out_shape=jax.ShapeDtypeStruct((T, LANES), I32),
        scratch_shapes=[pltpu.VMEM((SUBLANES, LANES), F32)],
        compiler_params=_params(("arbitrary",)),
        name="rank",
    )(idx_pad, ps_vec)


def _scatter_kernel(zrow_ref, dest_ref, hp_ref, xs_ref, zero_scr, sem, zsem, *, tm):
    first_step = pl.program_id(0) == 0
    n_e = N_EXPERTS

    def zero_fill(lo, hi, rows, start):
        def body(e, carry):
            @pl.when(zrow_ref[e] >= 0)
            def _():
                r0 = pl.multiple_of(zrow_ref[e], SUBLANES)
                cp = pltpu.make_async_copy(zero_scr.at[pl.ds(0, rows)], xs_ref.at[pl.ds(r0, rows)], zsem)
                if start:
                    cp.start()
                else:
                    cp.wait()
            return carry

        lax.fori_loop(lo, hi, body, 0)

    @pl.when(first_step)
    def _():
        zero_scr[...] = jnp.zeros_like(zero_scr)
        zero_fill(0, n_e, ROW_STEP, True)
        zero_fill(0, n_e, ROW_STEP, False)
        zero_fill(n_e, 4 * n_e, ROW_STEP, True)
        zero_fill(4 * n_e, 5 * n_e, EXPERT_TILE, True)

    def issue(t, carry):
        for k in range(TOP_K):
            d = dest_ref[t * TOP_K + k]
            pltpu.make_async_copy(hp_ref.at[pl.ds(t, 1)], xs_ref.at[pl.ds(d, 1)], sem).start(priority=k % 2)
        return carry

    for t in range(tm):
        issue(t, 0)
    for k in range(TOP_K):
        pltpu.make_async_copy(hp_ref, xs_ref.at[pl.ds(0, tm)], sem).wait()

    @pl.when(first_step)
    def _():
        zero_fill(n_e, 4 * n_e, ROW_STEP, False)
        zero_fill(4 * n_e, 5 * n_e, EXPERT_TILE, False)


def scatter_rows(zrow, dest_flat, hp, n_rows, *, tm=512):
    T, W = hp.shape
    return pl.pallas_call(
        functools.partial(_scatter_kernel, tm=tm),
        grid_spec=pltpu.PrefetchScalarGridSpec(
            num_scalar_prefetch=1,
            grid=(T // tm,),
            in_specs=[
                pl.BlockSpec((tm * TOP_K,), lambda i, zr: (i,), memory_space=pltpu.SMEM),
                pl.BlockSpec((tm, W), lambda i, zr: (i, 0)),
            ],
            out_specs=pl.BlockSpec(memory_space=pl.ANY),
            scratch_shapes=[
                pltpu.VMEM((EXPERT_TILE, W), hp.dtype),
                pltpu.SemaphoreType.DMA(()),
                pltpu.SemaphoreType.DMA(()),
            ],
        ),
        out_shape=jax.ShapeDtypeStruct((n_rows, W), hp.dtype),
        compiler_params=_params(("arbitrary",)),
        name="scatter_rows",
    )(zrow, dest_flat, hp)


def _tile_changed(te_ref, i):
    return (i == 0) | (te_ref[i] != te_ref[jnp.maximum(i - 1, 0)])


def _stream_expert_weights(te_ref, tr_ref, ne_ref, fetch, cast, nj):
    j = pl.program_id(0)
    i = pl.program_id(1)

    @pl.when((tr_ref[i] > 0) & _tile_changed(te_ref, i))
    def _():
        @pl.when((j == 0) & (i == 0))
        def _():
            for c in fetch(te_ref[0], 0):
                c.start()

        for c in fetch(te_ref[i], j):
            c.wait()
        cast()
        wrap = ne_ref[i] < 0
        e_next = jnp.where(wrap, te_ref[0], ne_ref[i])
        j_next = jnp.where(wrap, j + 1, j)

        @pl.when(j_next < nj)
        def _():
            for c in fetch(e_next, j_next):
                c.start()


def _for_row_count(rows, out_ref, compute):
    for m in range(ROW_STEP, EXPERT_TILE + 1, ROW_STEP):
        @pl.when((rows > m - ROW_STEP) & (rows <= m))
        def _(m=m):
            compute(m)

    @pl.when(rows == 0)
    def _():
        out_ref[...] = jnp.zeros_like(out_ref)


def _swiglu_rows(xs_ref, wg_scr, wu_scr, bg_ref, bu_ref, h_ref, m):
    xb = xs_ref[:m, :].astype(BF16)
    tn = h_ref.shape[1]
    for n0 in range(0, tn, MXU_N):
        cols = slice(n0, n0 + MXU_N)

        def proj(w_scr, b_ref):
            return jnp.dot(xb, w_scr[:, cols], preferred_element_type=F32) + b_ref[0, :, cols]

        gate = jnp.minimum(proj(wg_scr, bg_ref), SWIGLU_LIMIT)
        up = jnp.clip(proj(wu_scr, bu_ref), -SWIGLU_LIMIT, SWIGLU_LIMIT)
        glu = gate * jax.nn.sigmoid(SWIGLU_ALPHA * gate)
        h_ref[:m, cols] = ((up + 1.0) * glu).astype(h_ref.dtype)
    if m < h_ref.shape[0]:
        h_ref[m:, :] = jnp.zeros((h_ref.shape[0] - m, tn), h_ref.dtype)


def _expert_up_kernel(te_ref, tr_ref, ne_ref, nu_ref, xs_ref, w_hbm, bg_ref, bu_ref, h_ref,
                      stage, wg_scr, wu_scr, sems, *, nj):
    tn = h_ref.shape[1]
    F = nj * tn

    def fetch(e, j):
        c0 = pl.multiple_of(j * tn, tn)
        c1 = pl.multiple_of(F + j * tn, tn)
        return (pltpu.make_async_copy(w_hbm.at[e, :, pl.ds(c0, tn)], stage.at[0], sems.at[0]),
                pltpu.make_async_copy(w_hbm.at[e, :, pl.ds(c1, tn)], stage.at[1], sems.at[1]))

    def cast():
        def chunk(c, carry):
            r = pl.ds(pl.multiple_of(c * CAST_ROWS, CAST_ROWS), CAST_ROWS)
            wg_scr[r, :] = stage[0, r, :].astype(BF16)
            wu_scr[r, :] = stage[1, r, :].astype(BF16)
            return carry

        lax.fori_loop(0, wg_scr.shape[0] // CAST_ROWS, chunk, 0)

    _stream_expert_weights(te_ref, tr_ref, ne_ref, fetch, cast, nj)
    rows = tr_ref[pl.program_id(1)]
    _for_row_count(rows, h_ref, functools.partial(_swiglu_rows, xs_ref, wg_scr, wu_scr, bg_ref, bu_ref, h_ref))


def expert_up(tile_e, tile_rows, next_e, n_used, xs, w_gate_up, b_gate_up3, *, tm=EXPERT_TILE, tn=1024):
    P, D = xs.shape
    F = w_gate_up.shape[2] // 2
    nj = F // tn
    row = lambda i, nu: jnp.minimum(i, nu[0] - 1)
    return pl.pallas_call(
        functools.partial(_expert_up_kernel, nj=nj),
        grid_spec=pltpu.PrefetchScalarGridSpec(
            num_scalar_prefetch=4,
            grid=(nj, P // tm),
            in_specs=[
                pl.BlockSpec((tm, D), lambda j, i, te, tr, ne, nu: (row(i, nu), 0)),
                pl.BlockSpec(memory_space=pl.ANY),
                pl.BlockSpec((1, 1, tn), lambda j, i, te, tr, ne, nu: (te[i], 0, j)),
                pl.BlockSpec((1, 1, tn), lambda j, i, te, tr, ne, nu: (te[i], 0, nj + j)),
            ],
            out_specs=pl.BlockSpec((tm, tn), lambda j, i, te, tr, ne, nu: (i, j)),
            scratch_shapes=[
                pltpu.VMEM((2, D, tn), F32),
                pltpu.VMEM((D, tn), BF16),
                pltpu.VMEM((D, tn), BF16),
                pltpu.SemaphoreType.DMA((2,)),
            ],
        ),
        out_shape=jax.ShapeDtypeStruct((P, F), BF16),
        compiler_params=_params(("arbitrary", "arbitrary")),
        name="expert_up",
    )(tile_e, tile_rows, next_e, n_used, xs, w_gate_up, b_gate_up3, b_gate_up3)


def _down_rows(h_ref, wd_scr, bd_ref, y_ref, m):
    y_ref[:m, :] = jnp.dot(h_ref[:m, :], wd_scr[...], preferred_element_type=F32) + bd_ref[0]
    if m < y_ref.shape[0]:
        y_ref[m:, :] = jnp.zeros((y_ref.shape[0] - m, y_ref.shape[1]), y_ref.dtype)


def _expert_dn_kernel(te_ref, tr_ref, ne_ref, nu_ref, h_ref, w_hbm, bd_ref, y_ref, stage, wd_scr, sem, *, nj):
    tn = y_ref.shape[1]

    def fetch(e, j):
        c0 = pl.multiple_of(j * tn, tn)
        return (pltpu.make_async_copy(w_hbm.at[e, :, pl.ds(c0, tn)], stage, sem),)

    def cast():
        def chunk(c, carry):
            r = pl.ds(pl.multiple_of(c * CAST_ROWS, CAST_ROWS), CAST_ROWS)
            wd_scr[r, :] = stage[r, :].astype(BF16)
            return carry

        lax.fori_loop(0, wd_scr.shape[0] // CAST_ROWS, chunk, 0)

    _stream_expert_weights(te_ref, tr_ref, ne_ref, fetch, cast, nj)
    rows = tr_ref[pl.program_id(1)]
    _for_row_count(rows, y_ref, functools.partial(_down_rows, h_ref, wd_scr, bd_ref, y_ref))


def expert_dn(tile_e, tile_rows, next_e, n_used, h, w_down, b_down3, *, tm=EXPERT_TILE, tn=2048):
    P, F = h.shape
    D = w_down.shape[2]
    nj = D // tn
    row = lambda i, nu: jnp.minimum(i, nu[0] - 1)
    return pl.pallas_call(
        functools.partial(_expert_dn_kernel, nj=nj),
        grid_spec=pltpu.PrefetchScalarGridSpec(
            num_scalar_prefetch=4,
            grid=(nj, P // tm),
            in_specs=[
                pl.BlockSpec((tm, F), lambda j, i, te, tr, ne, nu: (row(i, nu), 0)),
                pl.BlockSpec(memory_space=pl.ANY),
                pl.BlockSpec((1, 1, tn), lambda j, i, te, tr, ne, nu: (te[i], 0, j)),
            ],
            out_specs=pl.BlockSpec((tm, tn), lambda j, i, te, tr, ne, nu: (i, j)),
            scratch_shapes=[
                pltpu.VMEM((F, tn), F32),
                pltpu.VMEM((F, tn), BF16),
                pltpu.SemaphoreType.DMA(()),
            ],
        ),
        out_shape=jax.ShapeDtypeStruct((P, D), F32),
        compiler_params=_params(("arbitrary", "arbitrary")),
        name="expert_dn",
    )(tile_e, tile_rows, next_e, n_used, h, w_down, b_down3)


def _combine_kernel(dcur_ref, dnxt_ref, x1_ref, gate_ref, gfin_ref, y_ref, o_ref, buf, sems, *, tm):
    i = pl.program_id(0)
    n = pl.num_programs(0)

    def gather(dest_ref, h, unrolled):
        def issue(t, carry):
            for k in range(TOP_K):
                d = dest_ref[(h * tm + t) * TOP_K + k]
                pltpu.make_async_copy(y_ref.at[pl.ds(d, 1)], buf.at[h, k, pl.ds(t, 1)],
                                      sems.at[h]).start(priority=k % 2)
            return carry

        if unrolled:
            for t in range(tm):
                issue(t, 0)
        else:
            lax.fori_loop(0, tm, issue, 0, unroll=8)

    @pl.when(i == 0)
    def _():
        gather(dcur_ref, 0, False)
        gather(dcur_ref, 1, False)

    for h in range(2):
        rows = slice(h * tm, (h + 1) * tm)
        for k in range(TOP_K):
            pltpu.make_async_copy(y_ref.at[pl.ds(0, tm)], buf.at[h, k], sems.at[h]).wait()
        gates = gate_ref[rows, :]
        moe = gates[:, 0:1] * buf[h, 0]
        for k in range(1, TOP_K):
            moe = moe + gates[:, k:k + 1] * buf[h, k]
        o_ref[rows, :] = _rms(x1_ref[rows, :] + moe, gfin_ref[...])

        @pl.when(i + 1 < n)
        def _(h=h):
            gather(dnxt_ref, h, True)


def combine(dest_flat, x1, gates, g_fin, y, *, tm=256):
    T, D = x1.shape
    step = 2 * tm
    n = T // step
    blk = step * TOP_K
    return pl.pallas_call(
        functools.partial(_combine_kernel, tm=tm),
        grid=(n,),
        in_specs=[
            pl.BlockSpec((blk,), lambda i: (i,), memory_space=pltpu.SMEM),
            pl.BlockSpec((blk,), lambda i: (jnp.minimum(i + 1, n - 1),), memory_space=pltpu.SMEM),
            pl.BlockSpec((step, D), lambda i: (i, 0)),
            pl.BlockSpec((step, LANES), lambda i: (i, 0)),
            pl.BlockSpec((1, D), lambda i: (0, 0)),
            pl.BlockSpec(memory_space=pl.ANY),
        ],
        out_specs=pl.BlockSpec((step, D), lambda i: (i, 0)),
        out_shape=jax.ShapeDtypeStruct((T, D), F32),
        scratch_shapes=[pltpu.VMEM((2, TOP_K, tm, D), F32), pltpu.SemaphoreType.DMA((2,))],
        compiler_params=_params(("arbitrary",)),
        name="combine",
    )(dest_flat, dest_flat, x1, gates, g_fin, y)


def _tile_plan(counts, n_tiles, tile):
    tiles_e = (counts + tile - 1) // tile
    ends = jnp.cumsum(tiles_e)
    starts = ends - tiles_e
    pstart = starts * tile
    n_used = ends[-1]
    t = jnp.arange(n_tiles, dtype=I32)
    tc = jnp.minimum(t, n_used - 1)
    expert_of = lambda tiles: jnp.sum((tiles[:, None] >= ends[None, :]).astype(I32), axis=1)
    tile_e = expert_of(tc)
    mine = tile_e[:, None] == jnp.arange(counts.shape[0], dtype=I32)[None, :]
    pick = lambda table: jnp.sum(jnp.where(mine, table[None, :], 0), axis=1)
    left = pick(counts) - (tc - pick(starts)) * tile
    tile_rows = jnp.where(t < n_used, jnp.clip(left, 0, tile), 0)
    nxt = pick(ends)
    next_e = jnp.where(nxt < n_used, expert_of(nxt), -1)
    spare = n_used + jnp.arange(counts.shape[0], dtype=I32)
    window = jnp.where(counts % ROW_STEP != 0, pstart + (counts // ROW_STEP) * ROW_STEP, -1)
    behind = pstart + ((counts + ROW_STEP - 1) // ROW_STEP) * ROW_STEP
    blocks = [jnp.where(behind + b * ROW_STEP < ends * tile, behind + b * ROW_STEP, -1)
              for b in range(tile // ROW_STEP - 1)]
    zrow = jnp.concatenate([window] + blocks + [jnp.where(spare < n_tiles, spare * tile, -1)])
    return (pstart.astype(I32), tile_e.astype(I32), tile_rows.astype(I32), next_e.astype(I32),
            n_used.reshape(1).astype(I32), zrow.astype(I32))


def kernel(x, norm_mix_g, w_in, gla_w_gate_up, gla_b_gate, gla_norm_g, lru_conv_w, lru_conv_b, lru_w_a,
           lru_b_a, lru_w_i, lru_b_i, lru_a_param, w_out, norm_ffn_g, w_router, b_router, w_gate_up,
           b_gate_up, w_down, b_down, norm_final_g):
    B, S, D = x.shape
    assert w_in.shape[0] == 1, "the final norm is fused into the single layer's combine stage"
    T = B * S
    x2 = x.reshape(T, D)
    kw = GLA_HEADS * GLA_DK
    gw = GLA_HEADS * GLA_DV
    lw = LRU_HEADS * LRU_BLOCK
    g0 = 2 * kw + 2 * gw

    w_in0 = w_in[0]
    w_main = jnp.concatenate([w_in0[:, :g0], w_in0[:, g0 + GLA_GATE_RANK:]], axis=1).astype(BF16)
    w_glow = jnp.pad(w_in0[:, g0:g0 + GLA_GATE_RANK], ((0, 0), (0, LANES - GLA_GATE_RANK))).astype(BF16)
    z, g_low = in_proj(x2, norm_mix_g[0].reshape(1, D), w_main, w_glow)

    wgu_pad = jnp.pad(gla_w_gate_up[0], ((0, LANES - GLA_GATE_RANK), (0, 0))).astype(BF16)
    gla_o = gla(z, g_low, wgu_pad, gla_b_gate[0].reshape(1, kw), gla_norm_g[0].reshape(1, gw),
                batch=B, seq=S)
    lru_o = lru(z, lru_conv_w[0], lru_conv_b[0].reshape(1, lw), lru_w_a[0].astype(BF16),
                lru_b_a[0].reshape(1, lw), lru_w_i[0].astype(BF16), lru_b_i[0].reshape(1, lw),
                lru_a_param[0].reshape(1, lw), batch=B, seq=S, lx_col=g0)

    wr_pad = jnp.pad(w_router[0], ((0, 0), (0, LANES - N_EXPERTS)))
    wr_hi = wr_pad.astype(BF16)
    wr_lo = (wr_pad - wr_hi.astype(F32)).astype(BF16)
    br_pad = jnp.pad(b_router[0], (0, LANES - N_EXPERTS)).reshape(1, LANES)
    x1, hf, idx_pad, gates, cnt = out_proj(gla_o, lru_o, x2, w_out[0].astype(BF16),
                                           norm_ffn_g[0].reshape(1, D), wr_hi, wr_lo, br_pad)

    n_tiles = (T * TOP_K) // EXPERT_TILE + N_EXPERTS
    pstart, tile_e, tile_rows, next_e, n_used, zrow = _tile_plan(
        cnt[0, :N_EXPERTS].astype(I32), n_tiles, EXPERT_TILE)
    ps_vec = jnp.pad(pstart.astype(F32), (0, LANES - N_EXPERTS)).reshape(1, LANES)
    dest_pad = rank(idx_pad, ps_vec)
    dest_flat = dest_pad[:, :TOP_K].reshape(-1)

    xs = scatter_rows(zrow, dest_flat, hf, n_tiles * EXPERT_TILE)
    E, _, F2 = w_gate_up.shape[1:]
    h = expert_up(tile_e, tile_rows, next_e, n_used, xs, w_gate_up[0], b_gate_up[0].reshape(E, 1, F2))
    y = expert_dn(tile_e, tile_rows, next_e, n_used, h, w_down[0], b_down[0].reshape(E, 1, D))
    out = combine(dest_flat, x1, gates, norm_final_g.reshape(1, D), y)
    return out.reshape(B, S, D)
```
